```python
import math
import jax
import jax.numpy as jnp
from jax import lax
import numpy as np

D_MODEL = 1024
BATCH = 8
SEQ = 8192
DEPTH = 2

GRID_W = 64
CTX_LEN = 256
RMS_EPS = 1e-6

LRU_WIDTH = D_MODEL // 2
LRU_HEADS = 8
LRU_HEAD_DIM = LRU_WIDTH // LRU_HEADS
LRU_CONV = 4
CONV_LEFT = LRU_CONV // 2
CONV_RIGHT = LRU_CONV - 1 - CONV_LEFT
LRU_C = 8.0

S5_WIDTH = D_MODEL // 2
S5_GROUP = 16
S5_GROUPS = S5_WIDTH // S5_GROUP
S5_STATE = 64

REC_IN = 2 * LRU_WIDTH + S5_WIDTH
REC_MIX = LRU_WIDTH + S5_WIDTH

NA_HEADS = 16
NA_HEAD_DIM = D_MODEL // NA_HEADS
NA_KR = 8
NA_KC = 16
NEG_INF = -1e30

MOE_GROUPS = 4
MOE_PER_GROUP = 8
MOE_EXPERTS = MOE_GROUPS * MOE_PER_GROUP
MOE_TOP_K = 2
MOE_HIDDEN = 512
MOE_BLOCK = 1024

kernel_name = 'hybrid_rglru_s5_natten_hmoe_dit'


def _rmsnorm(x, g):
    xf = x.astype(jnp.float32)
    xf = xf * lax.rsqrt(jnp.mean(xf * xf, axis=-1, keepdims=True) + RMS_EPS)
    return xf.astype(x.dtype) * g


def _modulate(h, shift, scale):
    return h * (1 + scale) + shift


def _dwconv(u, w, b):
    out = lax.conv_general_dilated(u, w[:, None, :], window_strides=(1,),
                                   padding=[(CONV_LEFT, CONV_RIGHT)],
                                   dimension_numbers=('NWC', 'WIO', 'NWC'),
                                   feature_group_count=u.shape[-1])
    return out + b


def _real_combine(e1, e2):
    a1, b1 = e1
    a2, b2 = e2
    return a1 * a2, a2 * b1 + b2


def _real_scan(a, b, h0):
    if h0 is not None:
        b = b.at[:, 0].add(a[:, 0] * h0)
    return lax.associative_scan(_real_combine, (a, b), axis=1)[1]


def _cplx_combine(e1, e2):
    ar1, ai1, br1, bi1 = e1
    ar2, ai2, br2, bi2 = e2
    return (ar2 * ar1 - ai2 * ai1, ar2 * ai1 + ai2 * ar1,
            ar2 * br1 - ai2 * bi1 + br2, ar2 * bi1 + ai2 * br1 + bi2)


def _cplx_scan(a_re, a_im, b_re, b_im, h0):
    if h0 is not None:
        h_re, h_im = h0
        b_re = b_re.at[:, 0].add(a_re[:, 0] * h_re - a_im[:, 0] * h_im)
        b_im = b_im.at[:, 0].add(a_re[:, 0] * h_im + a_im[:, 0] * h_re)
    _, _, s_re, s_im = lax.associative_scan(_cplx_combine, (a_re, a_im, b_re, b_im), axis=1)
    return s_re, s_im


def _rglru_dir(u_c, u_l, wa, ba, wx, bx, lam, reverse):
    def coeffs(u):
        uf = u.astype(jnp.float32)
        uh = uf.reshape(uf.shape[:-1] + (LRU_HEADS, LRU_HEAD_DIM))
        r = jax.nn.sigmoid(jnp.einsum('bthi,hij->bthj', uh, wa.astype(jnp.float32)).reshape(uf.shape) + ba)
        i = jax.nn.sigmoid(jnp.einsum('bthi,hij->bthj', uh, wx.astype(jnp.float32)).reshape(uf.shape) + bx)
        log_a = -LRU_C * r * jax.nn.softplus(-lam.astype(jnp.float32))
        return jnp.exp(log_a), jnp.sqrt(-jnp.expm1(2.0 * log_a)) * (i * uf)
    if reverse:
        u_c, u_l = jnp.flip(u_c, 1), jnp.flip(u_l, 1)
    a_c, b_c = coeffs(u_c)
    h_c = _real_scan(a_c, b_c, None)
    a_l, b_l = coeffs(u_l)
    h_l = _real_scan(a_l, b_l, h_c[:, -1])
    if reverse:
        h_c, h_l = jnp.flip(h_c, 1), jnp.flip(h_l, 1)
    return h_c, h_l


def _s5_dir(u_c, u_l, a_re, a_im, log_dt, b_re, b_im, c_re, c_im, reverse):
    a_re = a_re.astype(jnp.float32)
    a_im = a_im.astype(jnp.float32)
    dt = jnp.exp(log_dt.astype(jnp.float32))[:, None]
    mag = jnp.exp(a_re * dt)
    lb_re = mag * jnp.cos(a_im * dt)
    lb_im = mag * jnp.sin(a_im * dt)
    den = a_re * a_re + a_im * a_im
    q_re = ((lb_re - 1.0) * a_re + lb_im * a_im) / den
    q_im = (lb_im * a_re - (lb_re - 1.0) * a_im) / den
    b_re = b_re.astype(jnp.float32)
    b_im = b_im.astype(jnp.float32)
    bb_re = q_re[..., None] * b_re - q_im[..., None] * b_im
    bb_im = q_re[..., None] * b_im + q_im[..., None] * b_re
    c_re = c_re.astype(jnp.float32)
    c_im = c_im.astype(jnp.float32)

    def run(u, h0):
        t = u.shape[1]
        d_re = jnp.einsum('btgp,gnp->btgn', u, bb_re)
        d_im = jnp.einsum('btgp,gnp->btgn', u, bb_im)
        a_r = jnp.broadcast_to(lb_re, (1, t) + lb_re.shape)
        a_i = jnp.broadcast_to(lb_im, (1, t) + lb_im.shape)
        return _cplx_scan(a_r, a_i, d_re, d_im, h0)

    def readout(h_re, h_im):
        y = jnp.einsum('gpn,btgn->btgp', c_re, h_re) - jnp.einsum('gpn,btgn->btgp', c_im, h_im)
        return jnp.flip(y, 1) if reverse else y

    if reverse:
        u_c, u_l = jnp.flip(u_c, 1), jnp.flip(u_l, 1)
    hc_re, hc_im = run(u_c, None)
    hl_re, hl_im = run(u_l, (hc_re[:, -1], hc_im[:, -1]))
    return readout(hc_re, hc_im), readout(hl_re, hl_im)


def _recurrent_mixer(h_l, h_c, need_ctx, w_in, conv_w, conv_b, lru_wa, lru_ba, lru_wx, lru_bx,
                     lru_lambda, s5_a_re, s5_a_im, s5_log_dt, s5_b_re, s5_b_im, s5_c_re, s5_c_im,
                     s5_d, s5_glu_w, s5_glu_b, w_out):
    xa_l, ga_l, ub_l = jnp.split(h_l @ w_in, [LRU_WIDTH, 2 * LRU_WIDTH], axis=-1)
    xa_c, ga_c, ub_c = jnp.split(h_c @ w_in, [LRU_WIDTH, 2 * LRU_WIDTH], axis=-1)
    xa_l = _dwconv(xa_l, conv_w, conv_b)
    xa_c = _dwconv(xa_c, conv_w, conv_b)
    hf_c, hf_l = _rglru_dir(xa_c, xa_l, lru_wa[0], lru_ba[0], lru_wx[0], lru_bx[0], lru_lambda[0], False)
    hb_c, hb_l = _rglru_dir(xa_c, xa_l, lru_wa[1], lru_ba[1], lru_wx[1], lru_bx[1], lru_lambda[1], True)
    def groups(u):
        return u.astype(jnp.float32).reshape(u.shape[:-1] + (S5_GROUPS, S5_GROUP))
    u_c, u_l = groups(ub_c), groups(ub_l)
    sf_c, sf_l = _s5_dir(u_c, u_l, s5_a_re[0], s5_a_im[0], s5_log_dt[0], s5_b_re[0], s5_b_im[0],
                         s5_c_re[0], s5_c_im[0], False)
    sb_c, sb_l = _s5_dir(u_c, u_l, s5_a_re[1], s5_a_im[1], s5_log_dt[1], s5_b_re[1], s5_b_im[1],
                         s5_c_re[1], s5_c_im[1], True)

    def merge(ga, h_f, h_b, ub, s_f, s_b):
        y_a = (h_f + h_b).astype(ga.dtype) * jax.nn.gelu(ga)
        y_s = (s_f + s_b).reshape(ub.shape).astype(ub.dtype) + s5_d * ub
        y_s = jax.nn.gelu(y_s)
        y_s = y_s * jax.nn.sigmoid(y_s @ s5_glu_w + s5_glu_b)
        return jnp.concatenate([y_a, y_s], axis=-1) @ w_out

    y_l = merge(ga_l, hf_l, hb_l, ub_l, sf_l, sb_l)
    y_c = merge(ga_c, hf_c, hb_c, ub_c, sf_c, sb_c) if need_ctx else None
    return y_l, y_c


def _na_mixer(h_l, h_c, need_ctx, w_qkv, w_out, rpb):
    bsz, seq_len, _ = h_l.shape
    rows = seq_len // GRID_W
    kr = min(NA_KR, rows)
    scale = NA_HEAD_DIM ** -0.5
    grid = (bsz, rows, GRID_W, NA_HEADS, NA_HEAD_DIM)
    q, k, v = jnp.split(h_l @ w_qkv, 3, axis=-1)
    q = (q * scale).reshape(grid)
    k = k.reshape(grid)
    v = v.reshape(grid)
    ctx_shape = (bsz, h_c.shape[1], NA_HEADS, NA_HEAD_DIM)
    k_c, v_c = jnp.split(h_c @ w_qkv[:, D_MODEL:], 2, axis=-1)
    k_c = k_c.reshape(ctx_shape)
    v_c = v_c.reshape(ctx_shape)
    col = jnp.arange(GRID_W)
    c_start = jnp.clip(col - NA_KC // 2, 0, GRID_W - NA_KC)
    col_ok = (col[None, :] >= c_start[:, None]) & (col[None, :] < c_start[:, None] + NA_KC)
    dc_idx = jnp.clip(col[None, :] - col[:, None] + NA_KC - 1, 0, 2 * NA_KC - 2)
    rpb_cols = rpb[:, :, dc_idx].astype(jnp.float32)
    n_loc = kr * GRID_W

    def row_block(args):
        r, q_r = args
        r_start = jnp.clip(r - kr // 2, 0, rows - kr)
        k_b = lax.dynamic_slice_in_dim(k, r_start, kr, axis=1)
        v_b = lax.dynamic_slice_in_dim(v, r_start, kr, axis=1)
        dr_idx = r_start + jnp.arange(kr) - r + NA_KR - 1
        bias = jnp.transpose(jnp.take(rpb_cols, dr_idx, axis=1), (0, 2, 1, 3))
        s_loc = jnp.einsum('bqhd,bkwhd->bhqkw', q_r, k_b).astype(jnp.float32) + bias
        s_loc = jnp.where(col_ok[:, None, :], s_loc, NEG_INF)
        s_ctx = jnp.einsum('bqhd,bchd->bhqc', q_r, k_c).astype(jnp.float32)
        s = jnp.concatenate([s_loc.reshape(bsz, NA_HEADS, GRID_W, n_loc), s_ctx], axis=-1)
        p = jax.nn.softmax(s, axis=-1).astype(v.dtype)
        p_loc = p[..., :n_loc].reshape(bsz, NA_HEADS, GRID_W, kr, GRID_W)
        return (jnp.einsum('bhqkw,bkwhd->bqhd', p_loc, v_b)
                + jnp.einsum('bhqc,bchd->bqhd', p[..., n_loc:], v_c))

    o = lax.map(row_block, (jnp.arange(rows), jnp.moveaxis(q, 1, 0)))
    o = jnp.moveaxis(o, 0, 1).reshape(bsz, seq_len, D_MODEL)
    y_l = o @ w_out
    y_c = None
    if need_ctx:
        q_c = (h_c @ w_qkv[:, :D_MODEL]).reshape(ctx_shape) * scale
        p_c = jax.nn.softmax(jnp.einsum('bqhd,bkhd->bhqk', q_c, k_c).astype(jnp.float32), axis=-1)
        o_c = jnp.einsum('bhqk,bkhd->bqhd', p_c.astype(v_c.dtype), v_c)
        y_c = o_c.reshape(bsz, h_c.shape[1], D_MODEL) @ w_out
    return y_l, y_c


def _moe(h, r1_w, r1_b, r2_w, r2_b, w_gate, w_up, w_down):
    tokens = h.reshape(-1, D_MODEL)
    blk = math.gcd(tokens.shape[0], MOE_BLOCK)

    def block(tb):
        p1 = jax.nn.softmax((tb @ r1_w + r1_b).astype(jnp.float32), axis=-1)
        g_val, g_idx = lax.top_k(p1, 1)
        l2 = (jnp.einsum('nd,gde->nge', tb, r2_w) + r2_b).astype(jnp.float32)
        l2 = jnp.einsum('nge,ng->ne', l2, jax.nn.one_hot(g_idx[:, 0], MOE_GROUPS, dtype=jnp.float32))
        e_val, e_idx = lax.top_k(l2, MOE_TOP_K)
        w = jax.nn.softmax(e_val, axis=-1) * g_val
        eid = g_idx * MOE_PER_GROUP + e_idx
        gates = jnp.einsum('nk,nke->ne', w, jax.nn.one_hot(eid, MOE_EXPERTS, dtype=jnp.float32))
        hid = jax.nn.silu(jnp.einsum('nd,edf->nef', tb, w_gate)) * jnp.einsum('nd,edf->nef', tb, w_up)
        return jnp.einsum('nef,efd->nd', hid * gates.astype(hid.dtype)[:, :, None], w_down)

    return lax.map(block, tokens.reshape(-1, blk, D_MODEL)).reshape(h.shape)


def setup_inputs(seed: int = 0) -> dict:
    key = jax.random.key(seed)
    keys = jax.random.split(key, 40)
    f32 = jnp.float32
    n_even = (DEPTH + 1) // 2
    n_odd = DEPTH // 2

    def nrm(i, shape, scale):
        return jax.random.normal(keys[i], shape, f32) * scale

    u = jax.random.uniform(keys[15], (n_even, 2, LRU_WIDTH), f32, 0.9, 0.999)
    a = u ** (1.0 / LRU_C)
    lru_lambda = jnp.log(a) - jnp.log1p(-a)
    s5_shape = (n_even, 2, S5_GROUPS, S5_STATE)
    return {
        'x': nrm(0, (BATCH, SEQ, D_MODEL), 1.0),
        'c': nrm(1, (BATCH, D_MODEL), 1.0),
        'ctx': nrm(2, (BATCH, CTX_LEN, D_MODEL), 1.0),
        'c_ctx': nrm(3, (D_MODEL,), 1.0),
        'ada_w': nrm(4, (DEPTH, D_MODEL, 6 * D_MODEL), 0.3 * D_MODEL ** -0.5),
        'ada_b': nrm(5, (DEPTH, 6 * D_MODEL), 0.02),
        'norm1_g': 1.0 + nrm(6, (DEPTH, D_MODEL), 0.02),
        'norm2_g': 1.0 + nrm(7, (DEPTH, D_MODEL), 0.02),
        'rec_w_in': nrm(8, (n_even, D_MODEL, REC_IN), D_MODEL ** -0.5),
        'rec_conv_w': nrm(9, (n_even, LRU_CONV, LRU_WIDTH), 0.5),
        'rec_conv_b': nrm(10, (n_even, LRU_WIDTH), 0.02),
        'lru_wa': nrm(11, (n_even, 2, LRU_HEADS, LRU_HEAD_DIM, LRU_HEAD_DIM), LRU_HEAD_DIM ** -0.5),
        'lru_ba': nrm(12, (n_even, 2, LRU_WIDTH), 0.02),
        'lru_wx': nrm(13, (n_even, 2, LRU_HEADS, LRU_HEAD_DIM, LRU_HEAD_DIM), LRU_HEAD_DIM ** -0.5),
        'lru_bx': nrm(14, (n_even, 2, LRU_WIDTH), 0.02),
        'lru_lambda': lru_lambda,
        's5_a_re': -0.5 + nrm(16, s5_shape, 0.01),
        's5_a_im': jnp.pi * jnp.arange(S5_STATE, dtype=f32) + nrm(17, s5_shape, 0.01),
        's5_log_dt': jax.random.uniform(keys[18], (n_even, 2, S5_GROUPS), f32, math.log(1e-3), math.log(1e-1)),
        's5_b_re': nrm(19, (n_even, 2, S5_GROUPS, S5_STATE, S5_GROUP), (2 * S5_GROUP) ** -0.5),
        's5_b_im': nrm(20, (n_even, 2, S5_GROUPS, S5_STATE, S5_GROUP), (2 * S5_GROUP) ** -0.5),
        's5_c_re': nrm(21, (n_even, 2, S5_GROUPS, S5_GROUP, S5_STATE), S5_STATE ** -0.5),
        's5_c_im': nrm(22, (n_even, 2, S5_GROUPS, S5_GROUP, S5_STATE), S5_STATE ** -0.5),
        's5_d': nrm(23, (n_even, S5_WIDTH), 0.5),
        's5_glu_w': nrm(24, (n_even, S5_WIDTH, S5_WIDTH), S5_WIDTH ** -0.5),
        's5_glu_b': nrm(25, (n_even, S5_WIDTH), 0.02),
        'rec_w_out': nrm(26, (n_even, REC_MIX, D_MODEL), REC_MIX ** -0.5),
        'na_w_qkv': nrm(27, (n_odd, D_MODEL, 3 * D_MODEL), D_MODEL ** -0.5),
        'na_w_out': nrm(28, (n_odd, D_MODEL, D_MODEL), D_MODEL ** -0.5),
        'na_rpb': nrm(29, (n_odd, NA_HEADS, 2 * NA_KR - 1, 2 * NA_KC - 1), 0.1),
        'moe_r1_w': nrm(30, (DEPTH, D_MODEL, MOE_GROUPS), D_MODEL ** -0.5),
        'moe_r1_b': nrm(31, (DEPTH, MOE_GROUPS), 0.01),
        'moe_r2_w': nrm(32, (DEPTH, MOE_GROUPS, D_MODEL, MOE_PER_GROUP), D_MODEL ** -0.5),
        'moe_r2_b': nrm(33, (DEPTH, MOE_GROUPS, MOE_PER_GROUP), 0.01),
        'moe_w_gate': nrm(34, (DEPTH, MOE_EXPERTS, D_MODEL, MOE_HIDDEN), D_MODEL ** -0.5),
        'moe_w_up': nrm(35, (DEPTH, MOE_EXPERTS, D_MODEL, MOE_HIDDEN), D_MODEL ** -0.5),
        'moe_w_down': nrm(36, (DEPTH, MOE_EXPERTS, MOE_HIDDEN, D_MODEL), MOE_HIDDEN ** -0.5),
        'final_norm_g': 1.0 + nrm(37, (D_MODEL,), 0.02),
    }


def reference(x, c, ctx, c_ctx, ada_w, ada_b, norm1_g, norm2_g, rec_w_in, rec_conv_w, rec_conv_b,
              lru_wa, lru_ba, lru_wx, lru_bx, lru_lambda, s5_a_re, s5_a_im, s5_log_dt, s5_b_re,
              s5_b_im, s5_c_re, s5_c_im, s5_d, s5_glu_w, s5_glu_b, rec_w_out, na_w_qkv, na_w_out,
              na_rpb, moe_r1_w, moe_r1_b, moe_r2_w, moe_r2_b, moe_w_gate, moe_w_up, moe_w_down,
              final_norm_g):
    silu_c = jax.nn.silu(c)
    silu_cc = jax.nn.silu(c_ctx)
    x_l, x_c = x, ctx
    for layer in range(DEPTH):
        need_ctx = layer < DEPTH - 1
        i = layer // 2
        mod_l = [m[:, None, :] for m in jnp.split(silu_c @ ada_w[layer] + ada_b[layer], 6, axis=-1)]
        mod_c = jnp.split(silu_cc @ ada_w[layer] + ada_b[layer], 6, axis=-1)
        h_l = _modulate(_rmsnorm(x_l, norm1_g[layer]), mod_l[0], mod_l[1])
        h_c = _modulate(_rmsnorm(x_c, norm1_g[layer]), mod_c[0], mod_c[1])
        if layer % 2 == 0:
            y_l, y_c = _recurrent_mixer(h_l, h_c, need_ctx, rec_w_in[i], rec_conv_w[i], rec_conv_b[i],
                                        lru_wa[i], lru_ba[i], lru_wx[i], lru_bx[i], lru_lambda[i],
                                        s5_a_re[i], s5_a_im[i], s5_log_dt[i], s5_b_re[i], s5_b_im[i],
                                        s5_c_re[i], s5_c_im[i], s5_d[i], s5_glu_w[i], s5_glu_b[i],
                                        rec_w_out[i])
        else:
            y_l, y_c = _na_mixer(h_l, h_c, need_ctx, na_w_qkv[i], na_w_out[i], na_rpb[i])
        moe_args = (moe_r1_w[layer], moe_r1_b[layer], moe_r2_w[layer], moe_r2_b[layer],
                    moe_w_gate[layer], moe_w_up[layer], moe_w_down[layer])
        x_l = x_l + mod_l[2] * y_l
        x_l = x_l + mod_l[5] * _moe(_modulate(_rmsnorm(x_l, norm2_g[layer]), mod_l[3], mod_l[4]), *moe_args)
        if need_ctx:
            x_c = x_c + mod_c[2] * y_c
            x_c = x_c + mod_c[5] * _moe(_modulate(_rmsnorm(x_c, norm2_g[layer]), mod_c[3], mod_c[4]), *moe_args)
    return _rmsnorm(x_l, final_norm_g)
```

```python
import functools

import jax
import jax.numpy as jnp
from jax import lax
from jax.experimental import pallas as pl
from jax.experimental.pallas import tpu as pltpu

F32 = jnp.float32
BF16 = jnp.bfloat16
HIGHEST = lax.Precision.HIGHEST

RMS_EPS = 1e-6
GRID_W = 64
LRU_HEADS = 8
LRU_C = 8.0
S5_GROUP = 16
S5_CHUNK = 16
NA_HEADS = 16
NA_KR = 8
NA_KC = 16
NEG_INF = -1e30
MOE_GROUPS = 4
MOE_PER_GROUP = 8
MOE_EXPERTS = MOE_GROUPS * MOE_PER_GROUP
MOE_PAIRS = MOE_GROUPS * (MOE_PER_GROUP * (MOE_PER_GROUP - 1) // 2)
EXPERT_TILE = 256
ROUTE_LANES = 128
TOKEN_TILE = 512
VMEM_LIMIT = 56 * 1024 * 1024


def _cparams(*sem):
    return pltpu.CompilerParams(dimension_semantics=sem, vmem_limit_bytes=VMEM_LIMIT)


def _norm_mod(x, g, shift, scale):
    ms = jnp.mean(x * x, axis=-1, keepdims=True)
    return (x * lax.rsqrt(ms + RMS_EPS)) * g * (1.0 + scale) + shift


def _silu(x):
    return x * jax.nn.sigmoid(x)


def _gelu(x):
    return jax.nn.gelu(x, approximate=True)


def _bdot(a, b):
    return jnp.dot(a.astype(BF16), b, preferred_element_type=F32)


def _ada_kernel(c_ref, w_ref, b_ref, o_ref):
    o_ref[0] = jnp.dot(_silu(c_ref[...]), w_ref[0], preferred_element_type=F32,
                       precision=HIGHEST) + b_ref[0]


def _ada_mod(cc, ada_w, ada_b):
    n_layers, d, d6 = ada_w.shape
    r = cc.shape[0]
    return pl.pallas_call(
        _ada_kernel,
        grid=(n_layers, d6 // d),
        in_specs=[pl.BlockSpec((r, d), lambda l, j: (0, 0)),
                  pl.BlockSpec((1, d, d), lambda l, j: (l, 0, j)),
                  pl.BlockSpec((1, 1, d), lambda l, j: (l, 0, j))],
        out_specs=pl.BlockSpec((1, r, d), lambda l, j: (l, 0, j)),
        out_shape=jax.ShapeDtypeStruct((n_layers, r, d6), F32),
        compiler_params=_cparams("arbitrary", "arbitrary"),
        name="ada_mod",
    )(cc, ada_w, ada_b.reshape(n_layers, 1, d6))


def _inproj_kernel(x_ref, g_ref, sh_ref, sc_ref, w_ref, xa_ref, ga_ref, ub_ref):
    h = _norm_mod(x_ref[0], g_ref[...], sh_ref[0], sc_ref[0])
    r = _bdot(h, w_ref[...])
    w = xa_ref.shape[-1]
    xa_ref[0] = r[:, :w]
    ga_ref[0] = r[:, w:2 * w]
    ub_ref[0] = r[:, 2 * w:]


def _inproj(x, g, shift, scale, w_in):
    b, t, d = x.shape
    w = w_in.shape[1] // 3
    tm = min(TOKEN_TILE, t)
    row = pl.BlockSpec((1, 1, d), lambda bi, i: (bi, 0, 0))
    out = pl.BlockSpec((1, tm, w), lambda bi, i: (bi, i, 0))
    return pl.pallas_call(
        _inproj_kernel,
        grid=(b, t // tm),
        in_specs=[pl.BlockSpec((1, tm, d), lambda bi, i: (bi, i, 0)),
                  pl.BlockSpec((1, d), lambda bi, i: (0, 0)), row, row,
                  pl.BlockSpec(w_in.shape, lambda bi, i: (0, 0))],
        out_specs=[out, out, out],
        out_shape=[jax.ShapeDtypeStruct((b, t, w), F32)] * 3,
        compiler_params=_cparams("parallel", "parallel"),
        name="inproj",
    )(x, g, shift, scale, w_in)


def _lru_kernel(xc_ref, xp_ref, xn_ref, cw_ref, cb_ref, wa_ref, ba_ref, wx_ref, bx_ref, lam_ref,
                h0_ref, o_ref, ext_ref, a_ref, b_ref, car_ref, *, reverse, nt, tt):
    i = pl.program_id(1)
    ti = (nt - 1 - i) if reverse else i
    w = o_ref.shape[-1]

    @pl.when(i == 0)
    def _():
        car_ref[...] = h0_ref[0]

    ext_ref[0:8, :] = jnp.where(ti == 0, 0.0, xp_ref[0])
    ext_ref[8:8 + tt, :] = xc_ref[0]
    ext_ref[8 + tt:16 + tt, :] = jnp.where(ti == nt - 1, 0.0, xn_ref[0])
    cw = cw_ref[...]
    u = (ext_ref[6:6 + tt, :] * cw[0:1] + ext_ref[7:7 + tt, :] * cw[1:2]
         + ext_ref[8:8 + tt, :] * cw[2:3] + ext_ref[9:9 + tt, :] * cw[3:4]) + cb_ref[...]
    r = jax.nn.sigmoid(_bdot(u, wa_ref[...]) + ba_ref[...])
    ig = jax.nn.sigmoid(_bdot(u, wx_ref[...]) + bx_ref[...])
    log_a = (-LRU_C) * r * jax.nn.softplus(-lam_ref[...])
    a = jnp.exp(log_a)
    a_ref[...] = a
    b_ref[...] = jnp.sqrt(-jnp.tanh(log_a) * (1.0 + a * a)) * (ig * u)

    nsl = tt // 8
    row = lax.broadcasted_iota(jnp.int32, (8, w), 0)

    def slab(s, carry):
        off = pl.multiple_of(((nsl - 1 - s) if reverse else s) * 8, 8)
        a = a_ref[pl.ds(off, 8), :]
        bb = b_ref[pl.ds(off, 8), :]
        for k in (1, 2, 4):
            valid = (row < 8 - k) if reverse else (row >= k)
            sh = (8 - k) if reverse else k
            a_s = jnp.where(valid, pltpu.roll(a, sh, 0), 1.0)
            b_s = jnp.where(valid, pltpu.roll(bb, sh, 0), 0.0)
            bb = bb + a * b_s
            a = a * a_s
        h = bb + a * carry
        o_ref[0, pl.ds(off, 8), :] = h
        return h[0:1] if reverse else h[7:8]

    car_ref[...] = lax.fori_loop(0, nsl, slab, car_ref[...])


def _lru_dir(xa, conv_w, conv_b, wa_bd, ba, wx_bd, bx, lam, h0, reverse):
    b, t, w = xa.shape
    tt = min(TOKEN_TILE, t)
    nt = t // tt
    hb = tt // 8
    tile = (lambda i: nt - 1 - i) if reverse else (lambda i: i)
    full = lambda shape: pl.BlockSpec(shape, lambda bi, i: (0,) * len(shape))
    return pl.pallas_call(
        functools.partial(_lru_kernel, reverse=reverse, nt=nt, tt=tt),
        grid=(b, nt),
        in_specs=[pl.BlockSpec((1, tt, w), lambda bi, i: (bi, tile(i), 0)),
                  pl.BlockSpec((1, 8, w), lambda bi, i: (bi, jnp.maximum(tile(i) * hb - 1, 0), 0)),
                  pl.BlockSpec((1, 8, w), lambda bi, i: (bi, jnp.minimum((tile(i) + 1) * hb, t // 8 - 1), 0)),
                  full(conv_w.shape), full((1, w)), full((w, w)), full((1, w)), full((w, w)),
                  full((1, w)), full((1, w)),
                  pl.BlockSpec((1, 1, w), lambda bi, i: (bi, 0, 0))],
        out_specs=pl.BlockSpec((1, tt, w), lambda bi, i: (bi, tile(i), 0)),
        out_shape=jax.ShapeDtypeStruct((b, t, w), F32),
        scratch_shapes=[pltpu.VMEM((tt + 16, w), F32), pltpu.VMEM((tt, w), F32),
                        pltpu.VMEM((tt, w), F32), pltpu.VMEM((1, w), F32)],
        compiler_params=_cparams("parallel", "arbitrary"),
        name="lru_bwd" if reverse else "lru_fwd",
    )(xa, xa, xa, conv_w, conv_b, wa_bd, ba, wx_bd, bx, lam, h0)


def _block_diag(w):
    h, d, _ = w.shape
    eye = jnp.eye(h, dtype=w.dtype)
    return (eye[:, None, :, None] * w[:, :, None, :]).reshape(h * d, h * d)


def _s5_tables(a_re, a_im, log_dt, b_re, b_im, c_re, c_im):
    L = S5_CHUNK
    g, n = a_re.shape[1], a_re.shape[2]
    p = b_re.shape[-1]
    f = lambda x: x.astype(F32)
    a_re, a_im, b_re, b_im, c_re, c_im = map(f, (a_re, a_im, b_re, b_im, c_re, c_im))
    dt = jnp.exp(f(log_dt))[..., None]
    mag = jnp.exp(a_re * dt)
    lb_re, lb_im = mag * jnp.cos(a_im * dt), mag * jnp.sin(a_im * dt)
    den = a_re * a_re + a_im * a_im
    q_re = ((lb_re - 1.0) * a_re + lb_im * a_im) / den
    q_im = (lb_im * a_re - (lb_re - 1.0) * a_im) / den
    bb_re = q_re[..., None] * b_re - q_im[..., None] * b_im
    bb_im = q_re[..., None] * b_im + q_im[..., None] * b_re
    pw_re, pw_im = [jnp.ones_like(lb_re)], [jnp.zeros_like(lb_im)]
    for _ in range(L):
        r_, i_ = pw_re[-1], pw_im[-1]
        pw_re.append(r_ * lb_re - i_ * lb_im)
        pw_im.append(r_ * lb_im + i_ * lb_re)
    pw_re, pw_im = jnp.stack(pw_re, 1), jnp.stack(pw_im, 1)
    es = functools.partial(jnp.einsum, precision=HIGHEST)
    kf = jnp.arange(L - 1, -1, -1)
    kb = jnp.arange(L)
    inc = []
    for d_, ks in ((0, kf), (1, kb)):
        pr, pi = pw_re[d_][ks], pw_im[d_][ks]
        inc.append((pr[..., None] * bb_re[d_] - pi[..., None] * bb_im[d_],
                    pr[..., None] * bb_im[d_] + pi[..., None] * bb_re[d_]))
    def wb_block(x):
        return jnp.transpose(x, (1, 0, 3, 2)).reshape(g // 2, 2, L, p, n)
    eye2 = jnp.eye(2, dtype=F32)
    def pair_cols(x):
        y = x[:, :, :, :, None, :] * eye2[None, :, None, None, :, None]
        return y.reshape(g // 2, 2 * L * p, 2 * n)
    wb = jnp.concatenate([pair_cols(wb_block(inc[0][0])), pair_cols(wb_block(inc[0][1])),
                          pair_cols(wb_block(inc[1][0])), pair_cols(wb_block(inc[1][1]))], axis=-1)
    rd = []
    for d_, ks in ((0, jnp.arange(1, L + 1)), (1, jnp.arange(L, 0, -1))):
        pr, pi = pw_re[d_][ks], pw_im[d_][ks]
        cl_re = c_re[d_][None] * pr[:, :, None, :] - c_im[d_][None] * pi[:, :, None, :]
        cl_im = c_re[d_][None] * pi[:, :, None, :] + c_im[d_][None] * pr[:, :, None, :]
        rd.append((cl_re, -cl_im))
    def wc_block(x):
        y = jnp.transpose(x, (1, 3, 0, 2)).reshape(g // 2, 2, n, L, p)
        y = y[:, :, :, None, :, :] * eye2[None, :, None, :, None, None]
        return y.reshape(g // 2, 2 * n, 2 * L * p)
    wc = jnp.concatenate([wc_block(rd[0][0]), wc_block(rd[0][1]),
                          wc_block(rd[1][0]), wc_block(rd[1][1])], axis=1)
    ker = []
    for d_ in (0, 1):
        pr, pi = pw_re[d_][:L], pw_im[d_][:L]
        cl_re = c_re[d_][None] * pr[:, :, None, :] - c_im[d_][None] * pi[:, :, None, :]
        cl_im = c_re[d_][None] * pi[:, :, None, :] + c_im[d_][None] * pr[:, :, None, :]
        ker.append(es('kgpn,gnq->kgpq', cl_re, bb_re[d_]) - es('kgpn,gnq->kgpq', cl_im, bb_im[d_]))
    s_i = jnp.arange(L)[:, None]
    t_i = jnp.arange(L)[None, :]
    kf_t = ker[0][jnp.clip(t_i - s_i, 0, L - 1)]
    kb_t = ker[1][jnp.clip(s_i - t_i, 0, L - 1)]
    m = (jnp.where((s_i <= t_i)[:, :, None, None, None], kf_t, 0.0)
         + jnp.where((s_i >= t_i)[:, :, None, None, None], kb_t, 0.0))
    mt = jnp.transpose(m, (2, 0, 4, 1, 3)).reshape(g, L * p, L * p)
    l16 = jnp.stack([jnp.stack([pw_re[d_][L], pw_im[d_][L]]) for d_ in (0, 1)])
    l16 = l16.reshape(2, 2, g // 2, 2 * n)
    return wb.astype(BF16), mt.astype(BF16), wc.astype(BF16), l16


def _s5_kernel(u_ref, wb_ref, m0_ref, m1_ref, wc_ref, l16_ref, h0_ref, y_ref, hfin_ref, sh_ref, *, nj):
    ph = pl.program_id(1)
    gp = pl.program_id(2)
    ngp = pl.num_programs(2)
    half = sh_ref.shape[-1] // 2
    q = half // 2

    @pl.when(ph == 0)
    def _():
        sh_ref[:, gp, :] = jnp.dot(u_ref[0, 0], wb_ref[0], preferred_element_type=F32)

        @pl.when(gp == ngp - 1)
        def _():
            lf_re, lf_im = l16_ref[0, 0], l16_ref[0, 1]
            lb_re, lb_im = l16_ref[1, 0], l16_ref[1, 1]

            def step(j, c):
                hf_re, hf_im, hb_re, hb_im = c
                jb = nj - 1 - j
                sf = sh_ref[j, :, 0:half]
                sh_ref[j, :, 0:half] = jnp.concatenate([hf_re, hf_im], axis=-1)
                sb = sh_ref[jb, :, half:2 * half]
                sh_ref[jb, :, half:2 * half] = jnp.concatenate([hb_re, hb_im], axis=-1)
                return (lf_re * hf_re - lf_im * hf_im + sf[:, :q], lf_re * hf_im + lf_im * hf_re + sf[:, q:],
                        lb_re * hb_re - lb_im * hb_im + sb[:, :q], lb_re * hb_im + lb_im * hb_re + sb[:, q:])

            h0 = h0_ref[0]
            c = lax.fori_loop(0, nj, step, (h0[:, 0:q], h0[:, q:2 * q], h0[:, 2 * q:3 * q], h0[:, 3 * q:]))
            hfin_ref[0] = jnp.concatenate(c, axis=-1)

    @pl.when(ph == 1)
    def _():
        u = u_ref[0, 0]
        hw = u.shape[-1] // 2
        y = jnp.concatenate([jnp.dot(u[:, :hw], m0_ref[0], preferred_element_type=F32),
                             jnp.dot(u[:, hw:], m1_ref[0], preferred_element_type=F32)], axis=-1)
        y_ref[0, 0] = y + _bdot(sh_ref[:, gp, :], wc_ref[0])


def _s5(u_gm, tables, h0):
    wb, mt, wc, l16 = tables
    b, ngp, nj, lanes = u_gm.shape
    last = ngp - 1
    return pl.pallas_call(
        functools.partial(_s5_kernel, nj=nj),
        grid=(b, 2, ngp),
        in_specs=[pl.BlockSpec((1, 1, nj, lanes), lambda bi, ph, g: (bi, g, 0, 0)),
                  pl.BlockSpec((1, lanes, lanes), lambda bi, ph, g: (g * (1 - ph) + last * ph, 0, 0)),
                  pl.BlockSpec((1, lanes // 2, lanes // 2), lambda bi, ph, g: (2 * g * ph, 0, 0)),
                  pl.BlockSpec((1, lanes // 2, lanes // 2), lambda bi, ph, g: (2 * g * ph + 1, 0, 0)),
                  pl.BlockSpec((1, lanes, lanes), lambda bi, ph, g: (g * ph, 0, 0)),
                  pl.BlockSpec(l16.shape, lambda bi, ph, g: (0, 0, 0, 0)),
                  pl.BlockSpec((1, ngp, lanes), lambda bi, ph, g: (bi, 0, 0))],
        out_specs=[pl.BlockSpec((1, 1, nj, lanes), lambda bi, ph, g: (bi, g * ph, 0, 0)),
                   pl.BlockSpec((1, ngp, lanes), lambda bi, ph, g: (bi, 0, 0))],
        out_shape=[jax.ShapeDtypeStruct((b, ngp, nj, lanes), F32),
                   jax.ShapeDtypeStruct((b, ngp, lanes), F32)],
        scratch_shapes=[pltpu.VMEM((nj, ngp, lanes), F32)],
        compiler_params=_cparams("parallel", "arbitrary", "arbitrary"),
        name="s5",
    )(u_gm, wb, mt, mt, wc, l16, h0)


def _to_group_major(ub):
    b, t, w = ub.shape
    g = w // S5_GROUP
    x = ub.astype(BF16).reshape(b, t // S5_CHUNK, S5_CHUNK, g // 2, 2, S5_GROUP)
    return jnp.transpose(x, (0, 3, 1, 4, 2, 5)).reshape(b, g // 2, t // S5_CHUNK, 2 * S5_CHUNK * S5_GROUP)


def _from_group_major(y, t):
    b, ngp, nj, _ = y.shape
    x = y.reshape(b, ngp, nj, 2, S5_CHUNK, S5_GROUP)
    return jnp.transpose(x, (0, 2, 4, 1, 3, 5)).reshape(b, t, ngp * 2 * S5_GROUP)


def _route(h, wr_ref, br_ref, rec_ref):
    logits = jnp.dot(h, wr_ref[...], preferred_element_type=F32, precision=HIGHEST) + br_ref[...]
    lane = lax.broadcasted_iota(jnp.int32, logits.shape, 1)
    big = jnp.int32(ROUTE_LANES)
    l1 = jnp.where(lane < MOE_GROUPS, logits, NEG_INF)
    m1 = jnp.max(l1, axis=-1, keepdims=True)
    gidx = jnp.min(jnp.where(l1 == m1, lane, big), axis=-1, keepdims=True)
    gval = 1.0 / jnp.sum(jnp.where(lane < MOE_GROUPS, jnp.exp(logits - m1), 0.0), axis=-1, keepdims=True)
    lo = MOE_GROUPS + MOE_PER_GROUP * gidx
    l2 = jnp.where((lane >= lo) & (lane < lo + MOE_PER_GROUP), logits, NEG_INF)
    v1 = jnp.max(l2, axis=-1, keepdims=True)
    i1 = jnp.min(jnp.where(l2 == v1, lane, big), axis=-1, keepdims=True)
    l2 = jnp.where(lane == i1, NEG_INF, l2)
    v2 = jnp.max(l2, axis=-1, keepdims=True)
    i2 = jnp.min(jnp.where(l2 == v2, lane, big), axis=-1, keepdims=True)
    e = jnp.exp(v2 - v1)
    wa = gval / (1.0 + e)
    wb = wa * e
    ea = (i1 - MOE_GROUPS).astype(F32)
    eb = (i2 - MOE_GROUPS).astype(F32)
    rec_ref[0] = jnp.where(lane == 0, wa, jnp.where(lane == 1, wb, jnp.where(lane == 2, ea,
                           jnp.where(lane == 3, eb, 0.0))))


def _router_tables(r1_w, r1_b, r2_w, r2_b):
    d = r1_w.shape[0]
    wr = jnp.zeros((d, ROUTE_LANES), F32)
    wr = wr.at[:, :MOE_GROUPS].set(r1_w)
    wr = wr.at[:, MOE_GROUPS:MOE_GROUPS + MOE_EXPERTS].set(jnp.transpose(r2_w, (1, 0, 2)).reshape(d, MOE_EXPERTS))
    br = jnp.zeros((1, ROUTE_LANES), F32)
    br = br.at[0, :MOE_GROUPS].set(r1_b)
    br = br.at[0, MOE_GROUPS:MOE_GROUPS + MOE_EXPERTS].set(r2_b.reshape(MOE_EXPERTS))
    return wr, br


def _merge_kernel(x_ref, ga_ref, hf_ref, hb_ref, ub_ref, ys_ref, d_ref, gw_ref, gb_ref, wo_ref, gate_ref,
                  g2_ref, sh_ref, sc_ref, wr_ref, br_ref, xo_ref, h_ref, rec_ref):
    w = ga_ref.shape[-1]
    y_a = (hf_ref[0] + hb_ref[0]) * _gelu(ga_ref[0])
    y_s = _gelu(ys_ref[0] + d_ref[...] * ub_ref[0])
    y_s = y_s * jax.nn.sigmoid(_bdot(y_s, gw_ref[...]) + gb_ref[...])
    y = _bdot(y_a, wo_ref[0:w, :]) + _bdot(y_s, wo_ref[w:2 * w, :])
    x = x_ref[0] + gate_ref[0] * y
    xo_ref[0] = x
    h = _norm_mod(x, g2_ref[...], sh_ref[0], sc_ref[0])
    h_ref[0] = h
    _route(h, wr_ref, br_ref, rec_ref)


def _merge(x, ga, hf, hb, ub, ys, s5_d, glu_w, glu_b, w_out, gate, g2, shift, scale, wr, br):
    b, t, d = x.shape
    w = ga.shape[-1]
    tm = min(TOKEN_TILE, t)
    tok = lambda n: pl.BlockSpec((1, tm, n), lambda bi, i: (bi, i, 0))
    row = pl.BlockSpec((1, 1, d), lambda bi, i: (bi, 0, 0))
    full = lambda shape: pl.BlockSpec(shape, lambda bi, i: (0,) * len(shape))
    return pl.pallas_call(
        _merge_kernel,
        grid=(b, t // tm),
        in_specs=[tok(d), tok(w), tok(w), tok(w), tok(w), tok(w), full((1, w)), full((w, w)), full((1, w)),
                  full((2 * w, d)), row, full((1, d)), row, row, full(wr.shape), full(br.shape)],
        out_specs=[tok(d), tok(d), tok(ROUTE_LANES)],
        out_shape=[jax.ShapeDtypeStruct((b, t, d), F32), jax.ShapeDtypeStruct((b, t, d), F32),
                   jax.ShapeDtypeStruct((b, t, ROUTE_LANES), F32)],
        compiler_params=_cparams("parallel", "parallel"),
        name="merge_route",
    )(x, ga, hf, hb, ub, ys, s5_d, glu_w, glu_b, w_out, gate, g2, shift, scale, wr, br)


def _expert_kernel(elo_ref, ehi_ref, nrows_ref, idx_ref, wlo_ref, whi_ref, h_hbm,
                   gl_ref, ul_ref, dl_ref, gh_ref, uh_ref, dh_ref, y_hbm, xbuf, ybuf, gsem, ssem):
    t = pl.program_id(0)
    n = nrows_ref[t]

    @pl.when(t == 0)
    def _():
        xbuf[...] = jnp.zeros_like(xbuf)

    def gather(r):
        return pltpu.make_async_copy(h_hbm.at[pl.ds(idx_ref[0, 0, r], 1)], xbuf.at[pl.ds(r, 1)], gsem)

    def scatter(r):
        return pltpu.make_async_copy(ybuf.at[pl.ds(r, 1)], y_hbm.at[pl.ds(idx_ref[0, 0, r], 1)], ssem)

    def each(fn):
        def body(r, c):
            fn(r)
            return c
        lax.fori_loop(0, n, body, 0)

    @pl.when(n > 0)
    def _():
        each(lambda r: gather(r).start())
        each(lambda r: gather(r).wait())
        x = xbuf[...].astype(BF16)

        def ffn(g_ref, u_ref, d_ref):
            hid = _silu(jnp.dot(x, g_ref[0], preferred_element_type=F32)) * jnp.dot(
                x, u_ref[0], preferred_element_type=F32)
            return _bdot(hid, d_ref[0])

        ybuf[...] = wlo_ref[0] * ffn(gl_ref, ul_ref, dl_ref) + whi_ref[0] * ffn(gh_ref, uh_ref, dh_ref)
        each(lambda r: scatter(r).start())
        each(lambda r: scatter(r).wait())


def _moe_schedule(rec, tm):
    n = rec.shape[0]
    wa, wb = rec[:, 0], rec[:, 1]
    ea = jnp.clip(rec[:, 2].astype(jnp.int32), 0, MOE_EXPERTS - 1)
    eb = jnp.clip(rec[:, 3].astype(jnp.int32), 0, MOE_EXPERTS - 1)
    first = ea <= eb
    lo, hi = jnp.where(first, ea, eb), jnp.where(first, eb, ea)
    wlo, whi = jnp.where(first, wa, wb), jnp.where(first, wb, wa)
    nbk = MOE_EXPERTS * MOE_EXPERTS
    bucket = lo * MOE_EXPERTS + hi
    order = jnp.argsort(bucket, stable=True).astype(jnp.int32)
    edges = jnp.searchsorted(bucket[order], jnp.arange(nbk + 1, dtype=jnp.int32), side="left").astype(jnp.int32)
    start, count = edges[:-1], edges[1:] - edges[:-1]
    tiles = (count + tm - 1) // tm
    tile_end = jnp.cumsum(tiles)
    nt = n // tm + MOE_PAIRS
    tix = jnp.arange(nt, dtype=jnp.int32)
    total = tile_end[-1]
    bk = jnp.searchsorted(tile_end, jnp.minimum(tix, total - 1), side="right").astype(jnp.int32)
    bk = jnp.clip(bk, 0, nbk - 1)
    k = tix - (tile_end[bk] - tiles[bk])
    nrows = jnp.where(tix < total, jnp.clip(count[bk] - k * tm, 0, tm), 0).astype(jnp.int32)
    pos = jnp.clip(start[bk][:, None] + k[:, None] * tm + jnp.arange(tm, dtype=jnp.int32)[None, :], 0, n - 1)
    idx = order[pos]
    return (bk // MOE_EXPERTS, bk % MOE_EXPERTS, nrows, idx.reshape(nt, 1, tm),
            wlo[idx].reshape(nt, tm, 1), whi[idx].reshape(nt, tm, 1))


def _experts(h, rec, w_gate, w_up, w_down):
    n, d = h.shape
    f = w_gate.shape[-1]
    tm = EXPERT_TILE
    elo, ehi, nrows, idx, wlo, whi = _moe_schedule(rec, tm)
    nt = nrows.shape[0]
    wspec = lambda shape, which: pl.BlockSpec(
        (1,) + shape, (lambda t, elo, ehi, nr: (elo[t], 0, 0)) if which == 0 else (lambda t, elo, ehi, nr: (ehi[t], 0, 0)))
    gs = pltpu.PrefetchScalarGridSpec(
        num_scalar_prefetch=3,
        grid=(nt,),
        in_specs=[pl.BlockSpec((1, 1, tm), lambda t, *_: (t, 0, 0), memory_space=pltpu.SMEM),
                  pl.BlockSpec((1, tm, 1), lambda t, *_: (t, 0, 0)),
                  pl.BlockSpec((1, tm, 1), lambda t, *_: (t, 0, 0)),
                  pl.BlockSpec(memory_space=pl.ANY),
                  wspec((d, f), 0), wspec((d, f), 0), wspec((f, d), 0),
                  wspec((d, f), 1), wspec((d, f), 1), wspec((f, d), 1)],
        out_specs=pl.BlockSpec(memory_space=pl.ANY),
        scratch_shapes=[pltpu.VMEM((tm, d), F32), pltpu.VMEM((tm, d), F32),
                        pltpu.SemaphoreType.DMA(()), pltpu.SemaphoreType.DMA(())])
    return pl.pallas_call(
        _expert_kernel,
        grid_spec=gs,
        out_shape=jax.ShapeDtypeStruct((n, d), F32),
        compiler_params=_cparams("arbitrary"),
        name="experts",
    )(elo, ehi, nrows, idx, wlo, whi, h, w_gate, w_up, w_down, w_gate, w_up, w_down)


def _qkv_kernel(x_ref, y_ref, gate_ref, g_ref, sh_ref, sc_ref, w_ref, xo_ref, q_ref, k_ref, v_ref, *, qscale):
    x = x_ref[0] + gate_ref[0] * y_ref[0]
    xo_ref[0] = x
    h = _norm_mod(x, g_ref[...], sh_ref[0], sc_ref[0])
    r = _bdot(h, w_ref[...])
    d = x.shape[-1]
    npair = q_ref.shape[1]
    lanes = q_ref.shape[-1]
    for p in range(npair):
        q_ref[0, p] = (r[:, p * lanes:(p + 1) * lanes] * qscale).astype(BF16)
        k_ref[0, p] = r[:, d + p * lanes:d + (p + 1) * lanes].astype(BF16)
        v_ref[0, p] = r[:, 2 * d + p * lanes:2 * d + (p + 1) * lanes].astype(BF16)


def _qkv(x, y, gate, g, shift, scale, w_qkv):
    b, t, d = x.shape
    tm = min(TOKEN_TILE, t)
    npair = NA_HEADS // 2
    lanes = d // npair
    tok = pl.BlockSpec((1, tm, d), lambda bi, i: (bi, i, 0))
    row = pl.BlockSpec((1, 1, d), lambda bi, i: (bi, 0, 0))
    hp = pl.BlockSpec((1, npair, tm, lanes), lambda bi, i: (bi, 0, i, 0))
    hps = jax.ShapeDtypeStruct((b, npair, t, lanes), BF16)
    return pl.pallas_call(
        functools.partial(_qkv_kernel, qscale=float((d // NA_HEADS) ** -0.5)),
        grid=(b, t // tm),
        in_specs=[tok, tok, row, pl.BlockSpec((1, d), lambda bi, i: (0, 0)), row, row,
                  pl.BlockSpec(w_qkv.shape, lambda bi, i: (0, 0))],
        out_specs=[tok, hp, hp, hp],
        out_shape=[jax.ShapeDtypeStruct((b, t, d), F32), hps, hps, hps],
        compiler_params=_cparams("parallel", "parallel"),
        name="qkv",
    )(x, y, gate, g, shift, scale, w_qkv)


def _na_bias_tables(rpb):
    h = rpb.shape[0]
    col = jnp.arange(GRID_W)
    c_start = jnp.clip(col - NA_KC // 2, 0, GRID_W - NA_KC)
    col_ok = (col[None, :] >= c_start[:, None]) & (col[None, :] < c_start[:, None] + NA_KC)
    dc_idx = jnp.clip(col[None, :] - col[:, None] + NA_KC - 1, 0, 2 * NA_KC - 2)
    rc = jnp.where(col_ok[None, None], rpb[:, :, dc_idx].astype(F32), NEG_INF)
    dv = jnp.arange(NA_KR)[:, None]
    kr = jnp.arange(NA_KR)[None, :]
    t = rc[:, kr - dv + NA_KR - 1]
    t = jnp.transpose(t, (0, 1, 3, 2, 4)).reshape(h // 2, 2, NA_KR, GRID_W, NA_KR * GRID_W)
    return t


def _attn_kernel(q_ref, kp_ref, kc_ref, kn_ref, vp_ref, vc_ref, vn_ref, kx_ref, vx_ref, bias_ref, o_ref,
                 kbuf, vbuf, *, rows):
    i = pl.program_id(2)
    blk = kc_ref.shape[2]
    kbuf[0:blk, :] = kp_ref[0, 0]
    kbuf[blk:2 * blk, :] = kc_ref[0, 0]
    kbuf[2 * blk:3 * blk, :] = kn_ref[0, 0]
    vbuf[0:blk, :] = vp_ref[0, 0]
    vbuf[blk:2 * blk, :] = vc_ref[0, 0]
    vbuf[2 * blk:3 * blk, :] = vn_ref[0, 0]
    kx = kx_ref[0, 0]
    vx = vx_ref[0, 0]
    lanes = q_ref.shape[-1]
    lane = lax.broadcasted_iota(jnp.int32, (GRID_W, lanes), 1)
    nt_dims = (((1,), (1,)), ((), ()))
    for rho in range(NA_KR):
        r = NA_KR * i + rho
        rs = jnp.clip(r - NA_KR // 2, 0, rows - NA_KR)
        wstart = pl.multiple_of((rs - NA_KR * i + NA_KR) * GRID_W, GRID_W)
        dvar = r - rs
        kw = kbuf[pl.ds(wstart, blk), :]
        vw = vbuf[pl.ds(wstart, blk), :]
        q = q_ref[0, 0, rho * GRID_W:(rho + 1) * GRID_W, :]
        acc = None
        for hh in range(2):
            mine = (lane < lanes // 2) if hh == 0 else (lane >= lanes // 2)
            qm = jnp.where(mine, q, jnp.zeros_like(q))
            s_loc = lax.dot_general(qm, kw, nt_dims, preferred_element_type=F32) + bias_ref[0, hh, dvar]
            s_ctx = lax.dot_general(qm, kx, nt_dims, preferred_element_type=F32)
            m = jnp.maximum(jnp.max(s_loc, axis=-1, keepdims=True), jnp.max(s_ctx, axis=-1, keepdims=True))
            p_loc = jnp.exp(s_loc - m)
            p_ctx = jnp.exp(s_ctx - m)
            den = jnp.sum(p_loc, axis=-1, keepdims=True) + jnp.sum(p_ctx, axis=-1, keepdims=True)
            o = (_bdot(p_loc, vw) + _bdot(p_ctx, vx)) / den
            acc = o if acc is None else jnp.where(mine, o, acc)
        o_ref[0, rho * GRID_W:(rho + 1) * GRID_W, :] = acc.astype(o_ref.dtype)


def _attention(q, k, v, kx, vx, bias):
    b, npair, t, lanes = q.shape
    c = kx.shape[2]
    blk = NA_KR * GRID_W
    nb = t // blk
    rows = t // GRID_W
    cur = lambda p, bi, i: (bi, p, i, 0)
    prv = lambda p, bi, i: (bi, p, jnp.maximum(i - 1, 0), 0)
    nxt = lambda p, bi, i: (bi, p, jnp.minimum(i + 1, nb - 1), 0)
    tb = lambda m: pl.BlockSpec((1, 1, blk, lanes), m)
    cx = pl.BlockSpec((1, 1, c, lanes), lambda p, bi, i: (bi, p, 0, 0))
    return pl.pallas_call(
        functools.partial(_attn_kernel, rows=rows),
        grid=(npair, b, nb),
        in_specs=[tb(cur), tb(prv), tb(cur), tb(nxt), tb(prv), tb(cur), tb(nxt), cx, cx,
                  pl.BlockSpec((1,) + bias.shape[1:], lambda p, bi, i: (p, 0, 0, 0, 0))],
        out_specs=pl.BlockSpec((1, blk, lanes), lambda p, bi, i: (bi, i, p)),
        out_shape=jax.ShapeDtypeStruct((b, t, npair * lanes), BF16),
        scratch_shapes=[pltpu.VMEM((3 * blk, lanes), BF16), pltpu.VMEM((3 * blk, lanes), BF16)],
        compiler_params=_cparams("parallel", "parallel", "parallel"),
        name="na_attention",
    )(q, k, k, k, v, v, v, kx, vx, bias)


def _oproj_kernel(x_ref, o_ref, wo_ref, gate_ref, g2_ref, sh_ref, sc_ref, wr_ref, br_ref, xo_ref, h_ref, rec_ref):
    x = x_ref[0] + gate_ref[0] * jnp.dot(o_ref[0], wo_ref[...], preferred_element_type=F32)
    xo_ref[0] = x
    h = _norm_mod(x, g2_ref[...], sh_ref[0], sc_ref[0])
    h_ref[0] = h
    _route(h, wr_ref, br_ref, rec_ref)


def _oproj(x, o, w_out, gate, g2, shift, scale, wr, br):
    b, t, d = x.shape
    tm = min(TOKEN_TILE, t)
    tok = lambda n: pl.BlockSpec((1, tm, n), lambda bi, i: (bi, i, 0))
    row = pl.BlockSpec((1, 1, d), lambda bi, i: (bi, 0, 0))
    full = lambda shape: pl.BlockSpec(shape, lambda bi, i: (0,) * len(shape))
    return pl.pallas_call(
        _oproj_kernel,
        grid=(b, t // tm),
        in_specs=[tok(d), tok(d), full((d, d)), row, full((1, d)), row, row, full(wr.shape), full(br.shape)],
        out_specs=[tok(d), tok(d), tok(ROUTE_LANES)],
        out_shape=[jax.ShapeDtypeStruct((b, t, d), F32), jax.ShapeDtypeStruct((b, t, d), F32),
                   jax.ShapeDtypeStruct((b, t, ROUTE_LANES), F32)],
        compiler_params=_cparams("parallel", "parallel"),
        name="oproj_route",
    )(x, o, w_out, gate, g2, shift, scale, wr, br)


def _final_kernel(x_ref, y_ref, gate_ref, g_ref, o_ref):
    x = x_ref[0] + gate_ref[0] * y_ref[0]
    ms = jnp.mean(x * x, axis=-1, keepdims=True)
    o_ref[0] = (x * lax.rsqrt(ms + RMS_EPS)) * g_ref[...]


def _final(x, y, gate, g):
    b, t, d = x.shape
    tm = min(TOKEN_TILE, t)
    tok = pl.BlockSpec((1, tm, d), lambda bi, i: (bi, i, 0))
    return pl.pallas_call(
        _final_kernel,
        grid=(b, t // tm),
        in_specs=[tok, tok, pl.BlockSpec((1, 1, d), lambda bi, i: (bi, 0, 0)),
                  pl.BlockSpec((1, d), lambda bi, i: (0, 0))],
        out_specs=tok,
        out_shape=jax.ShapeDtypeStruct((b, t, d), F32),
        compiler_params=_cparams("parallel", "parallel"),
        name="final_norm",
    )(x, y, gate, g)


def kernel(x, c, ctx, c_ctx, ada_w, ada_b, norm1_g, norm2_g, rec_w_in, rec_conv_w, rec_conv_b, lru_wa, lru_ba, lru_wx, lru_bx, lru_lambda, s5_a_re, s5_a_im, s5_log_dt, s5_b_re, s5_b_im, s5_c_re, s5_c_im, s5_d, s5_glu_w, s5_glu_b, rec_w_out, na_w_qkv, na_w_out, na_rpb, moe_r1_w, moe_r1_b, moe_r2_w, moe_r2_b, moe_w_gate, moe_w_up, moe_w_down, final_norm_g):
    b, t, d = x.shape
    tc = ctx.shape[1]
    assert ada_w.shape[0] == 2, "layer 0 recurrent mixer, layer 1 neighbourhood attention"
    w = rec_w_in.shape[-1] // 3

    rpad = -(b + 1) % 8
    cc = jnp.concatenate([c, c_ctx[None], jnp.zeros((rpad, d), F32)], axis=0)
    mod = _ada_mod(cc, ada_w, ada_b)

    def mods(layer, ctx_rows):
        rows = jnp.broadcast_to(mod[layer, b:b + 1], (b, 6 * d)) if ctx_rows else mod[layer, :b]
        return [rows[:, j * d:(j + 1) * d].reshape(b, 1, d) for j in range(6)]

    row = lambda v: v.reshape(1, -1)

    w_in = rec_w_in[0].astype(BF16)
    lru = [(_block_diag(lru_wa[0, dr]).astype(BF16), row(lru_ba[0, dr]), _block_diag(lru_wx[0, dr]).astype(BF16),
            row(lru_bx[0, dr]), row(lru_lambda[0, dr])) for dr in (0, 1)]
    s5t = _s5_tables(s5_a_re[0], s5_a_im[0], s5_log_dt[0], s5_b_re[0], s5_b_im[0], s5_c_re[0], s5_c_im[0])
    glu_w = s5_glu_w[0].astype(BF16)
    w_out0 = rec_w_out[0].astype(BF16)
    wr0, br0 = _router_tables(moe_r1_w[0], moe_r1_b[0], moe_r2_w[0], moe_r2_b[0])

    def mixer0(xs, m, h0_lru, h0_s5):
        xa, ga, ub = _inproj(xs, row(norm1_g[0]), m[0], m[1], w_in)
        hf = _lru_dir(xa, rec_conv_w[0], row(rec_conv_b[0]), *lru[0], h0_lru[0], False)
        hb = _lru_dir(xa, rec_conv_w[0], row(rec_conv_b[0]), *lru[1], h0_lru[1], True)
        ys_gm, s5_fin = _s5(_to_group_major(ub), s5t, h0_s5)
        ys = _from_group_major(ys_gm, xs.shape[1])
        x_mid, h2, rec = _merge(xs, ga, hf, hb, ub, ys, row(s5_d[0]), glu_w, row(s5_glu_b[0]), w_out0, m[2],
                                row(norm2_g[0]), m[3], m[4], wr0, br0)
        return x_mid, h2, rec, (hf[:, -1:], hb[:, :1]), s5_fin

    zl = jnp.zeros((b, 1, w), F32)
    zs = jnp.zeros((b, w // (2 * S5_GROUP), 2 * S5_CHUNK * S5_GROUP), F32)
    mc0, ml0 = mods(0, True), mods(0, False)
    xc_mid, hc2, recc, lru_fin, s5_fin = mixer0(ctx, mc0, (zl, zl), zs)
    xl_mid, hl2, recl, _, _ = mixer0(x, ml0, lru_fin, s5_fin)

    h_all = jnp.concatenate([hl2.reshape(b * t, d), hc2.reshape(b * tc, d)], axis=0)
    rec_all = jnp.concatenate([recl.reshape(b * t, -1), recc.reshape(b * tc, -1)], axis=0)
    y_all = _experts(h_all, rec_all, moe_w_gate[0].astype(BF16), moe_w_up[0].astype(BF16),
                     moe_w_down[0].astype(BF16))
    yl = y_all[:b * t].reshape(b, t, d)
    yc = y_all[b * t:].reshape(b, tc, d)

    w_qkv = na_w_qkv[0].astype(BF16)
    mc1, ml1 = mods(1, True), mods(1, False)
    _, _, kx, vx = _qkv(xc_mid, yc, mc0[5], row(norm1_g[1]), mc1[0], mc1[1], w_qkv)
    x1, q, k, v = _qkv(xl_mid, yl, ml0[5], row(norm1_g[1]), ml1[0], ml1[1], w_qkv)
    o = _attention(q, k, v, kx, vx, _na_bias_tables(na_rpb[0]))
    wr1, br1 = _router_tables(moe_r1_w[1], moe_r1_b[1], moe_r2_w[1], moe_r2_b[1])
    x1_mid, h2, rec = _oproj(x1, o, na_w_out[0].astype(BF16), ml1[2], row(norm2_g[1]), ml1[3], ml1[4], wr1, br1)
    y1 = _experts(h2.reshape(b * t, d), rec.reshape(b * t, -1), moe_w_gate[1].astype(BF16),
                  moe_w_up[1].astype(BF16), moe_w_down[1].astype(BF16))
    return _final(x1_mid, y1.reshape(b, t, d), ml1[5], row(final_norm_g))
```

```python
import functools

import jax
import jax.numpy as jnp
from jax import lax
from jax.experimental import pallas as pl
from jax.experimental.pallas import tpu as pltpu

F32 = jnp.float32
BF16 = jnp.bfloat16
HIGHEST = lax.Precision.HIGHEST

RMS_EPS = 1e-6
GRID_W = 64
LRU_HEADS = 8
LRU_C = 8.0
S5_GROUP = 16
S5_CHUNK = 16
NA_HEADS = 16
NA_KR = 8
NA_KC = 16
NEG_INF = -1e30
MOE_GROUPS = 4
MOE_PER_GROUP = 8
MOE_EXPERTS = MOE_GROUPS * MOE_PER_GROUP
MOE_PAIRS = MOE_GROUPS * (MOE_PER_GROUP * (MOE_PER_GROUP - 1) // 2)
EXPERT_TILE = 256
ROUTE_LANES = 128
TOKEN_TILE = 512
VMEM_LIMIT = 56 * 1024 * 1024


def _cparams(*sem):
    return pltpu.CompilerParams(dimension_semantics=sem, vmem_limit_bytes=VMEM_LIMIT)


def _norm_mod(x, g, shift, scale):
    ms = jnp.mean(x * x, axis=-1, keepdims=True)
    return (x * lax.rsqrt(ms + RMS_EPS)) * g * (1.0 + scale) + shift


def _silu(x):
    return x * jax.nn.sigmoid(x)


def _gelu(x):
    return jax.nn.gelu(x, approximate=True)


def _bdot(a, b):
    return jnp.dot(a.astype(BF16), b, preferred_element_type=F32)


def _ada_kernel(c_ref, w_ref, b_ref, o_ref):
    o_ref[0] = jnp.dot(_silu(c_ref[...]), w_ref[0], preferred_element_type=F32,
                       precision=HIGHEST) + b_ref[0]


def _ada_mod(cc, ada_w, ada_b):
    n_layers, d, d6 = ada_w.shape
    r = cc.shape[0]
    return pl.pallas_call(
        _ada_kernel,
        grid=(n_layers, d6 // d),
        in_specs=[pl.BlockSpec((r, d), lambda l, j: (0, 0)),
                  pl.BlockSpec((1, d, d), lambda l, j: (l, 0, j)),
                  pl.BlockSpec((1, 1, d), lambda l, j: (l, 0, j))],
        out_specs=pl.BlockSpec((1, r, d), lambda l, j: (l, 0, j)),
        out_shape=jax.ShapeDtypeStruct((n_layers, r, d6), F32),
        compiler_params=_cparams("arbitrary", "arbitrary"),
        name="ada_mod",
    )(cc, ada_w, ada_b.reshape(n_layers, 1, d6))


def _inproj_kernel(x_ref, g_ref, sh_ref, sc_ref, w_ref, xa_ref, ga_ref, ub_ref):
    h = _norm_mod(x_ref[0], g_ref[...], sh_ref[0], sc_ref[0])
    r = _bdot(h, w_ref[...])
    w = xa_ref.shape[-1]
    xa_ref[0] = r[:, :w]
    ga_ref[0] = r[:, w:2 * w]
    ub_ref[0] = r[:, 2 * w:]


def _inproj(x, g, shift, scale, w_in):
    b, t, d = x.shape
    w = w_in.shape[1] // 3
    tm = min(TOKEN_TILE, t)
    row = pl.BlockSpec((1, 1, d), lambda bi, i: (bi, 0, 0))
    out = pl.BlockSpec((1, tm, w), lambda bi, i: (bi, i, 0))
    return pl.pallas_call(
        _inproj_kernel,
        grid=(b, t // tm),
        in_specs=[pl.BlockSpec((1, tm, d), lambda bi, i: (bi, i, 0)),
                  pl.BlockSpec((1, d), lambda bi, i: (0, 0)), row, row,
                  pl.BlockSpec(w_in.shape, lambda bi, i: (0, 0))],
        out_specs=[out, out, out],
        out_shape=[jax.ShapeDtypeStruct((b, t, w), F32)] * 3,
        compiler_params=_cparams("parallel", "parallel"),
        name="inproj",
    )(x, g, shift, scale, w_in)


def _lru_kernel(xc_ref, xp_ref, xn_ref, cw_ref, cb_ref, wa_ref, ba_ref, wx_ref, bx_ref, lam_ref,
                h0_ref, o_ref, ext_ref, a_ref, b_ref, car_ref, *, reverse, nt, tt):
    i = pl.program_id(1)
    ti = (nt - 1 - i) if reverse else i
    w = o_ref.shape[-1]

    @pl.when(i == 0)
    def _():
        car_ref[...] = h0_ref[0]

    ext_ref[0:8, :] = jnp.where(ti == 0, 0.0, xp_ref[0])
    ext_ref[8:8 + tt, :] = xc_ref[0]
    ext_ref[8 + tt:16 + tt, :] = jnp.where(ti == nt - 1, 0.0, xn_ref[0])
    cw = cw_ref[...]
    u = (ext_ref[6:6 + tt, :] * cw[0:1] + ext_ref[7:7 + tt, :] * cw[1:2]
         + ext_ref[8:8 + tt, :] * cw[2:3] + ext_ref[9:9 + tt, :] * cw[3:4]) + cb_ref[...]
    r = jax.nn.sigmoid(_bdot(u, wa_ref[...]) + ba_ref[...])
    ig = jax.nn.sigmoid(_bdot(u, wx_ref[...]) + bx_ref[...])
    log_a = (-LRU_C) * r * jax.nn.softplus(-lam_ref[...])
    a = jnp.exp(log_a)
    a_ref[...] = a
    b_ref[...] = jnp.sqrt(-jnp.tanh(log_a) * (1.0 + a * a)) * (ig * u)

    nsl = tt // 8
    row = lax.broadcasted_iota(jnp.int32, (8, w), 0)

    def slab(s, carry):
        off = pl.multiple_of(((nsl - 1 - s) if reverse else s) * 8, 8)
        a = a_ref[pl.ds(off, 8), :]
        bb = b_ref[pl.ds(off, 8), :]
        for k in (1, 2, 4):
            valid = (row < 8 - k) if reverse else (row >= k)
            sh = (8 - k) if reverse else k
            a_s = jnp.where(valid, pltpu.roll(a, sh, 0), 1.0)
            b_s = jnp.where(valid, pltpu.roll(bb, sh, 0), 0.0)
            bb = bb + a * b_s
            a = a * a_s
        h = bb + a * carry
        o_ref[0, pl.ds(off, 8), :] = h
        return h[0:1] if reverse else h[7:8]

    car_ref[...] = lax.fori_loop(0, nsl, slab, car_ref[...])


def _lru_dir(xa, conv_w, conv_b, wa_bd, ba, wx_bd, bx, lam, h0, reverse):
    b, t, w = xa.shape
    tt = min(TOKEN_TILE, t)
    nt = t // tt
    hb = tt // 8
    tile = (lambda i: nt - 1 - i) if reverse else (lambda i: i)
    full = lambda shape: pl.BlockSpec(shape, lambda bi, i: (0,) * len(shape))
    return pl.pallas_call(
        functools.partial(_lru_kernel, reverse=reverse, nt=nt, tt=tt),
        grid=(b, nt),
        in_specs=[pl.BlockSpec((1, tt, w), lambda bi, i: (bi, tile(i), 0)),
                  pl.BlockSpec((1, 8, w), lambda bi, i: (bi, jnp.maximum(tile(i) * hb - 1, 0), 0)),
                  pl.BlockSpec((1, 8, w), lambda bi, i: (bi, jnp.minimum((tile(i) + 1) * hb, t // 8 - 1), 0)),
                  full(conv_w.shape), full((1, w)), full((w, w)), full((1, w)), full((w, w)),
                  full((1, w)), full((1, w)),
                  pl.BlockSpec((1, 1, w), lambda bi, i: (bi, 0, 0))],
        out_specs=pl.BlockSpec((1, tt, w), lambda bi, i: (bi, tile(i), 0)),
        out_shape=jax.ShapeDtypeStruct((b, t, w), F32),
        scratch_shapes=[pltpu.VMEM((tt + 16, w), F32), pltpu.VMEM((tt, w), F32),
                        pltpu.VMEM((tt, w), F32), pltpu.VMEM((1, w), F32)],
        compiler_params=_cparams("parallel", "arbitrary"),
        name="lru_bwd" if reverse else "lru_fwd",
    )(xa, xa, xa, conv_w, conv_b, wa_bd, ba, wx_bd, bx, lam, h0)


def _block_diag(w):
    h, d, _ = w.shape
    eye = jnp.eye(h, dtype=w.dtype)
    return (eye[:, None, :, None] * w[:, :, None, :]).reshape(h * d, h * d)


def _s5_tables(a_re, a_im, log_dt, b_re, b_im, c_re, c_im):
    L = S5_CHUNK
    g, n = a_re.shape[1], a_re.shape[2]
    p = b_re.shape[-1]
    f = lambda x: x.astype(F32)
    a_re, a_im, b_re, b_im, c_re, c_im = map(f, (a_re, a_im, b_re, b_im, c_re, c_im))
    dt = jnp.exp(f(log_dt))[..., None]
    mag = jnp.exp(a_re * dt)
    lb_re, lb_im = mag * jnp.cos(a_im * dt), mag * jnp.sin(a_im * dt)
    den = a_re * a_re + a_im * a_im
    q_re = ((lb_re - 1.0) * a_re + lb_im * a_im) / den
    q_im = (lb_im * a_re - (lb_re - 1.0) * a_im) / den
    bb_re = q_re[..., None] * b_re - q_im[..., None] * b_im
    bb_im = q_re[..., None] * b_im + q_im[..., None] * b_re
    pw_re, pw_im = [jnp.ones_like(lb_re)], [jnp.zeros_like(lb_im)]
    for _ in range(L):
        r_, i_ = pw_re[-1], pw_im[-1]
        pw_re.append(r_ * lb_re - i_ * lb_im)
        pw_im.append(r_ * lb_im + i_ * lb_re)
    pw_re, pw_im = jnp.stack(pw_re, 1), jnp.stack(pw_im, 1)
    es = functools.partial(jnp.einsum, precision=HIGHEST)
    kf = jnp.arange(L - 1, -1, -1)
    kb = jnp.arange(L)
    inc = []
    for d_, ks in ((0, kf), (1, kb)):
        pr, pi = pw_re[d_][ks], pw_im[d_][ks]
        inc.append((pr[..., None] * bb_re[d_] - pi[..., None] * bb_im[d_],
                    pr[..., None] * bb_im[d_] + pi[..., None] * bb_re[d_]))
    gq = 128 // p
    nq = g // gq
    eye = jnp.eye(gq, dtype=F32)
    inc_all = jnp.stack([jnp.stack(inc[0]), jnp.stack(inc[1])]).reshape(2, 2, L, nq, gq, n, p)
    wb = jnp.transpose(inc_all, (3, 2, 4, 6, 0, 1, 5))
    wb = wb[:, :, :, :, :, :, None, :] * eye[None, None, :, None, None, None, :, None]
    wb = wb.reshape(nq, L * gq * p, 4 * gq * n)
    rd = []
    for d_, ks in ((0, jnp.arange(1, L + 1)), (1, jnp.arange(L, 0, -1))):
        pr, pi = pw_re[d_][ks], pw_im[d_][ks]
        cl_re = c_re[d_][None] * pr[:, :, None, :] - c_im[d_][None] * pi[:, :, None, :]
        cl_im = c_re[d_][None] * pi[:, :, None, :] + c_im[d_][None] * pr[:, :, None, :]
        rd.append((cl_re, -cl_im))
    rd_all = jnp.stack([jnp.stack(rd[0]), jnp.stack(rd[1])]).reshape(2, 2, L, nq, gq, p, n)
    wc = jnp.transpose(rd_all, (3, 0, 1, 4, 6, 2, 5))
    wc = wc[:, :, :, :, :, :, None, :] * eye[None, None, None, :, None, None, :, None]
    wc = wc.reshape(nq, 4 * gq * n, L * gq * p)
    ker = []
    for d_ in (0, 1):
        pr, pi = pw_re[d_][:L], pw_im[d_][:L]
        cl_re = c_re[d_][None] * pr[:, :, None, :] - c_im[d_][None] * pi[:, :, None, :]
        cl_im = c_re[d_][None] * pi[:, :, None, :] + c_im[d_][None] * pr[:, :, None, :]
        ker.append(es('kgpn,gnq->kgpq', cl_re, bb_re[d_]) - es('kgpn,gnq->kgpq', cl_im, bb_im[d_]))
    s_i = jnp.arange(L)[:, None]
    t_i = jnp.arange(L)[None, :]
    kf_t = ker[0][jnp.clip(t_i - s_i, 0, L - 1)]
    kb_t = ker[1][jnp.clip(s_i - t_i, 0, L - 1)]
    m = (jnp.where((s_i <= t_i)[:, :, None, None, None], kf_t, 0.0)
         + jnp.where((s_i >= t_i)[:, :, None, None, None], kb_t, 0.0))
    mt = jnp.transpose(m.reshape(L, L, nq, gq, p, p), (2, 0, 3, 5, 1, 4))
    mt = mt[:, :, :, :, :, None, :] * eye[None, None, :, None, None, :, None]
    mt = mt.reshape(nq, L * gq * p, L * gq * p)
    l16 = jnp.stack([jnp.stack([pw_re[d_][L], pw_im[d_][L]]) for d_ in (0, 1)])
    l16 = l16.reshape(2, 2, nq, gq * n)
    return wb.astype(BF16), mt.astype(BF16), wc.astype(BF16), l16


def _s5_inc_kernel(x_ref, wb_ref, xs_ref, s_ref):
    lanes = x_ref.shape[-1]
    for tau in range(S5_CHUNK):
        xs_ref[0, 0, :, tau * lanes:(tau + 1) * lanes] = x_ref[0, :, tau, :].astype(BF16)
    s_ref[0, 0] = jnp.dot(xs_ref[0, 0], wb_ref[0], preferred_element_type=F32)


def _s5_scan_kernel(sf_ref, sb_ref, l16_ref, h0_ref, hf_ref, hb_ref, hfin_ref, st_ref, *, jb):
    i = pl.program_id(0)
    nq = st_ref.shape[0]
    dl = sf_ref.shape[-1]
    w = dl // 2

    @pl.when(i == 0)
    def _():
        st_ref[...] = h0_ref[...]

    for jj in range(jb):
        for q in range(nq):
            for d, (src, dst, row) in enumerate(((sf_ref, hf_ref, jj), (sb_ref, hb_ref, jb - 1 - jj))):
                s = src[q, :, row, :]
                h = st_ref[q, :, d * dl:(d + 1) * dl]
                dst[q, :, row, :] = h
                lr, li = l16_ref[d, 0, q:q + 1, :], l16_ref[d, 1, q:q + 1, :]
                hr, hi = h[:, :w], h[:, w:]
                st_ref[q, :, d * dl:(d + 1) * dl] = jnp.concatenate(
                    [lr * hr - li * hi + s[:, :w], lr * hi + li * hr + s[:, w:]], axis=-1)

    @pl.when(i == pl.num_programs(0) - 1)
    def _():
        hfin_ref[...] = st_ref[...]


def _s5_out_kernel(xs_ref, hf_ref, hb_ref, mt_ref, wc_ref, y_ref):
    dl = hf_ref.shape[-1]
    y = (jnp.dot(xs_ref[0, 0], mt_ref[0], preferred_element_type=F32)
         + _bdot(hf_ref[0, 0], wc_ref[0, 0:dl, :]) + _bdot(hb_ref[0, 0], wc_ref[0, dl:2 * dl, :]))
    lanes = y_ref.shape[-1]
    for k in range(y_ref.shape[2]):
        y_ref[0, :, k, :] = y[:, k * lanes:(k + 1) * lanes]


def _s5(ub, tables, h0):
    wb, mt, wc, l16 = tables
    b, t, w = ub.shape
    L = S5_CHUNK
    nj = t // L
    nq = wb.shape[0]
    lanes = w // nq
    cl = L * lanes
    sl = wb.shape[-1]
    x4 = ub.reshape(b, nj, L, w)
    xs, s = pl.pallas_call(
        _s5_inc_kernel,
        grid=(nq, b),
        in_specs=[pl.BlockSpec((1, nj, L, lanes), lambda q, bi: (bi, 0, 0, q)),
                  pl.BlockSpec((1, cl, sl), lambda q, bi: (q, 0, 0))],
        out_specs=[pl.BlockSpec((1, 1, nj, cl), lambda q, bi: (q, bi, 0, 0)),
                   pl.BlockSpec((1, 1, nj, sl), lambda q, bi: (q, bi, 0, 0))],
        out_shape=[jax.ShapeDtypeStruct((nq, b, nj, cl), BF16), jax.ShapeDtypeStruct((nq, b, nj, sl), F32)],
        compiler_params=_cparams("parallel", "parallel"),
        name="s5_inc",
    )(x4, wb)
    jb = 8
    nblk = nj // jb
    dl = sl // 2
    hf, hb, hfin = pl.pallas_call(
        functools.partial(_s5_scan_kernel, jb=jb),
        grid=(nblk,),
        in_specs=[pl.BlockSpec((nq, b, jb, dl), lambda i: (0, 0, i, 0)),
                  pl.BlockSpec((nq, b, jb, dl), lambda i: (0, 0, nblk - 1 - i, 1)),
                  pl.BlockSpec(l16.shape, lambda i: (0, 0, 0, 0)),
                  pl.BlockSpec((nq, b, sl), lambda i: (0, 0, 0))],
        out_specs=[pl.BlockSpec((nq, b, jb, dl), lambda i: (0, 0, i, 0)),
                   pl.BlockSpec((nq, b, jb, dl), lambda i: (0, 0, nblk - 1 - i, 0)),
                   pl.BlockSpec((nq, b, sl), lambda i: (0, 0, 0))],
        out_shape=[jax.ShapeDtypeStruct((nq, b, nj, dl), F32), jax.ShapeDtypeStruct((nq, b, nj, dl), F32),
                   jax.ShapeDtypeStruct((nq, b, sl), F32)],
        scratch_shapes=[pltpu.VMEM((nq, b, sl), F32)],
        compiler_params=_cparams("arbitrary"),
        name="s5_scan",
    )(s, s, l16, h0)
    nh = 2
    y = pl.pallas_call(
        _s5_out_kernel,
        grid=(nq, nh, b),
        in_specs=[pl.BlockSpec((1, 1, nj, cl), lambda q, h, bi: (q, bi, 0, 0)),
                  pl.BlockSpec((1, 1, nj, dl), lambda q, h, bi: (q, bi, 0, 0)),
                  pl.BlockSpec((1, 1, nj, dl), lambda q, h, bi: (q, bi, 0, 0)),
                  pl.BlockSpec((1, cl, cl // nh), lambda q, h, bi: (q, 0, h)),
                  pl.BlockSpec((1, sl, cl // nh), lambda q, h, bi: (q, 0, h))],
        out_specs=pl.BlockSpec((1, nj, L // nh, lanes), lambda q, h, bi: (bi, 0, h, q)),
        out_shape=jax.ShapeDtypeStruct((b, nj, L, w), F32),
        compiler_params=_cparams("parallel", "parallel", "parallel"),
        name="s5_out",
    )(xs, hf, hb, mt, wc)
    return y.reshape(b, t, w), hfin


def _route(h, wr_ref, br_ref, rec_ref):
    logits = jnp.dot(h, wr_ref[...], preferred_element_type=F32, precision=HIGHEST) + br_ref[...]
    lane = lax.broadcasted_iota(jnp.int32, logits.shape, 1)
    big = jnp.int32(ROUTE_LANES)
    l1 = jnp.where(lane < MOE_GROUPS, logits, NEG_INF)
    m1 = jnp.max(l1, axis=-1, keepdims=True)
    gidx = jnp.min(jnp.where(l1 == m1, lane, big), axis=-1, keepdims=True)
    gval = 1.0 / jnp.sum(jnp.where(lane < MOE_GROUPS, jnp.exp(logits - m1), 0.0), axis=-1, keepdims=True)
    lo = MOE_GROUPS + MOE_PER_GROUP * gidx
    l2 = jnp.where((lane >= lo) & (lane < lo + MOE_PER_GROUP), logits, NEG_INF)
    v1 = jnp.max(l2, axis=-1, keepdims=True)
    i1 = jnp.min(jnp.where(l2 == v1, lane, big), axis=-1, keepdims=True)
    l2 = jnp.where(lane == i1, NEG_INF, l2)
    v2 = jnp.max(l2, axis=-1, keepdims=True)
    i2 = jnp.min(jnp.where(l2 == v2, lane, big), axis=-1, keepdims=True)
    e = jnp.exp(v2 - v1)
    wa = gval / (1.0 + e)
    wb = wa * e
    ea = (i1 - MOE_GROUPS).astype(F32)
    eb = (i2 - MOE_GROUPS).astype(F32)
    rec_ref[0] = jnp.where(lane == 0, wa, jnp.where(lane == 1, wb, jnp.where(lane == 2, ea,
                           jnp.where(lane == 3, eb, 0.0))))


def _router_tables(r1_w, r1_b, r2_w, r2_b):
    d = r1_w.shape[0]
    wr = jnp.zeros((d, ROUTE_LANES), F32)
    wr = wr.at[:, :MOE_GROUPS].set(r1_w)
    wr = wr.at[:, MOE_GROUPS:MOE_GROUPS + MOE_EXPERTS].set(jnp.transpose(r2_w, (1, 0, 2)).reshape(d, MOE_EXPERTS))
    br = jnp.zeros((1, ROUTE_LANES), F32)
    br = br.at[0, :MOE_GROUPS].set(r1_b)
    br = br.at[0, MOE_GROUPS:MOE_GROUPS + MOE_EXPERTS].set(r2_b.reshape(MOE_EXPERTS))
    return wr, br


def _merge_kernel(x_ref, ga_ref, hf_ref, hb_ref, ub_ref, ys_ref, d_ref, gw_ref, gb_ref, wo_ref, gate_ref,
                  g2_ref, sh_ref, sc_ref, wr_ref, br_ref, xo_ref, h_ref, rec_ref):
    w = ga_ref.shape[-1]
    y_a = (hf_ref[0] + hb_ref[0]) * _gelu(ga_ref[0])
    y_s = _gelu(ys_ref[0] + d_ref[...] * ub_ref[0])
    y_s = y_s * jax.nn.sigmoid(_bdot(y_s, gw_ref[...]) + gb_ref[...])
    y = _bdot(y_a, wo_ref[0:w, :]) + _bdot(y_s, wo_ref[w:2 * w, :])
    x = x_ref[0] + gate_ref[0] * y
    xo_ref[0] = x
    h = _norm_mod(x, g2_ref[...], sh_ref[0], sc_ref[0])
    h_ref[0] = h
    _route(h, wr_ref, br_ref, rec_ref)


def _merge(x, ga, hf, hb, ub, ys, s5_d, glu_w, glu_b, w_out, gate, g2, shift, scale, wr, br):
    b, t, d = x.shape
    w = ga.shape[-1]
    tm = min(TOKEN_TILE, t)
    tok = lambda n: pl.BlockSpec((1, tm, n), lambda bi, i: (bi, i, 0))
    row = pl.BlockSpec((1, 1, d), lambda bi, i: (bi, 0, 0))
    full = lambda shape: pl.BlockSpec(shape, lambda bi, i: (0,) * len(shape))
    return pl.pallas_call(
        _merge_kernel,
        grid=(b, t // tm),
        in_specs=[tok(d), tok(w), tok(w), tok(w), tok(w), tok(w), full((1, w)), full((w, w)), full((1, w)),
                  full((2 * w, d)), row, full((1, d)), row, row, full(wr.shape), full(br.shape)],
        out_specs=[tok(d), tok(d), tok(ROUTE_LANES)],
        out_shape=[jax.ShapeDtypeStruct((b, t, d), F32), jax.ShapeDtypeStruct((b, t, d), F32),
                   jax.ShapeDtypeStruct((b, t, ROUTE_LANES), F32)],
        compiler_params=_cparams("parallel", "parallel"),
        name="merge_route",
    )(x, ga, hf, hb, ub, ys, s5_d, glu_w, glu_b, w_out, gate, g2, shift, scale, wr, br)


DMA_GROUP = 8


def _expert_kernel(elo_ref, ehi_ref, nrows_ref, idx_ref, idxn_ref, wlo_ref, whi_ref, h_hbm,
                   gl_ref, ul_ref, dl_ref, gh_ref, uh_ref, dh_ref, y_hbm, xbuf, ybuf, gsem, ssem):
    t = pl.program_id(0)
    nt = pl.num_programs(0)
    slot = t % 2
    n = nrows_ref[t]
    g = DMA_GROUP

    def loop(count, body):
        lax.fori_loop(0, count, lambda k, c: (body(k), c)[1], 0)

    def gather_start(iref, cnt, s):
        def grp(k):
            for j in range(g):
                r = k * g + j
                pltpu.make_async_copy(h_hbm.at[pl.ds(iref[0, 0, r], 1)], xbuf.at[s, pl.ds(r, 1)],
                                      gsem.at[s]).start()
        loop((cnt + g - 1) // g, grp)

    def gather_wait(cnt, s):
        def grp(k):
            pltpu.make_async_copy(h_hbm.at[pl.ds(0, g)], xbuf.at[s, pl.ds(pl.multiple_of(k * g, g), g)],
                                  gsem.at[s]).wait()
        loop((cnt + g - 1) // g, grp)

    def scatter_row(r):
        return pltpu.make_async_copy(ybuf.at[pl.ds(r, 1)], y_hbm.at[pl.ds(idx_ref[0, 0, r], 1)], ssem)

    def scatter_start(cnt):
        def grp(k):
            for j in range(g):
                scatter_row(k * g + j).start()
        loop(cnt // g, grp)
        loop(cnt % g, lambda k: scatter_row((cnt // g) * g + k).start())

    def scatter_wait(cnt):
        def grp(k):
            pltpu.make_async_copy(ybuf.at[pl.ds(pl.multiple_of(k * g, g), g)], y_hbm.at[pl.ds(0, g)], ssem).wait()
        loop(cnt // g, grp)
        loop(cnt % g, lambda k: pltpu.make_async_copy(ybuf.at[pl.ds(k, 1)], y_hbm.at[pl.ds(0, 1)], ssem).wait())

    @pl.when(t == 0)
    def _():
        xbuf[...] = jnp.zeros_like(xbuf)
        gather_start(idx_ref, n, 0)

    @pl.when(t + 1 < nt)
    def _():
        gather_start(idxn_ref, nrows_ref[jnp.minimum(t + 1, nt - 1)], 1 - slot)

    gather_wait(n, slot)

    @pl.when(n > 0)
    def _():
        x = xbuf[slot].astype(BF16)

        def ffn(g_ref, u_ref, d_ref):
            hid = _silu(jnp.dot(x, g_ref[0], preferred_element_type=F32)) * jnp.dot(
                x, u_ref[0], preferred_element_type=F32)
            return _bdot(hid, d_ref[0])

        y = wlo_ref[0] * ffn(gl_ref, ul_ref, dl_ref) + whi_ref[0] * ffn(gh_ref, uh_ref, dh_ref)
        scatter_wait(nrows_ref[jnp.maximum(t - 1, 0)] * (t > 0).astype(jnp.int32))
        ybuf[...] = y
        scatter_start(n)

    @pl.when((t == nt - 1) | (nrows_ref[jnp.minimum(t + 1, nt - 1)] == 0))
    def _():
        scatter_wait(n)


def _moe_schedule(rec, tm):
    n = rec.shape[0]
    wa, wb = rec[:, 0], rec[:, 1]
    ea = jnp.clip(rec[:, 2].astype(jnp.int32), 0, MOE_EXPERTS - 1)
    eb = jnp.clip(rec[:, 3].astype(jnp.int32), 0, MOE_EXPERTS - 1)
    first = ea <= eb
    lo, hi = jnp.where(first, ea, eb), jnp.where(first, eb, ea)
    wlo, whi = jnp.where(first, wa, wb), jnp.where(first, wb, wa)
    nbk = MOE_EXPERTS * MOE_EXPERTS
    bucket = lo * MOE_EXPERTS + hi
    order = jnp.argsort(bucket, stable=True).astype(jnp.int32)
    eids = jnp.arange(MOE_EXPERTS, dtype=jnp.int32)[None, :]
    count = jnp.einsum('nl,nh->lh', (lo[:, None] == eids).astype(F32), (hi[:, None] == eids).astype(F32),
                       precision=HIGHEST).astype(jnp.int32).reshape(nbk)
    start = jnp.cumsum(count) - count
    tiles = (count + tm - 1) // tm
    tile_end = jnp.cumsum(tiles)
    nt = n // tm + MOE_PAIRS
    tix = jnp.arange(nt, dtype=jnp.int32)
    total = tile_end[-1]
    bk = jnp.sum((tile_end[None, :] <= jnp.minimum(tix, total - 1)[:, None]).astype(jnp.int32), axis=1)
    bk = jnp.clip(bk, 0, nbk - 1)
    k = tix - (tile_end[bk] - tiles[bk])
    nrows = jnp.where(tix < total, jnp.clip(count[bk] - k * tm, 0, tm), 0).astype(jnp.int32)
    pos = jnp.clip(start[bk][:, None] + k[:, None] * tm + jnp.arange(tm, dtype=jnp.int32)[None, :], 0, n - 1)
    idx = order[pos]
    return (bk // MOE_EXPERTS, bk % MOE_EXPERTS, nrows, idx.reshape(nt, 1, tm),
            wlo[idx].reshape(nt, tm, 1), whi[idx].reshape(nt, tm, 1))


def _experts(h, rec, w_gate, w_up, w_down):
    n, d = h.shape
    f = w_gate.shape[-1]
    tm = EXPERT_TILE
    elo, ehi, nrows, idx, wlo, whi = _moe_schedule(rec, tm)
    nt = nrows.shape[0]
    wspec = lambda shape, which: pl.BlockSpec(
        (1,) + shape, (lambda t, elo, ehi, nr: (elo[t], 0, 0)) if which == 0 else (lambda t, elo, ehi, nr: (ehi[t], 0, 0)))
    gs = pltpu.PrefetchScalarGridSpec(
        num_scalar_prefetch=3,
        grid=(nt,),
        in_specs=[pl.BlockSpec((1, 1, tm), lambda t, *_: (t, 0, 0), memory_space=pltpu.SMEM),
                  pl.BlockSpec((1, 1, tm), lambda t, *_: (jnp.minimum(t + 1, nt - 1), 0, 0), memory_space=pltpu.SMEM),
                  pl.BlockSpec((1, tm, 1), lambda t, *_: (t, 0, 0)),
                  pl.BlockSpec((1, tm, 1), lambda t, *_: (t, 0, 0)),
                  pl.BlockSpec(memory_space=pl.ANY),
                  wspec((d, f), 0), wspec((d, f), 0), wspec((f, d), 0),
                  wspec((d, f), 1), wspec((d, f), 1), wspec((f, d), 1)],
        out_specs=pl.BlockSpec(memory_space=pl.ANY),
        scratch_shapes=[pltpu.VMEM((2, tm, d), F32), pltpu.VMEM((tm, d), F32),
                        pltpu.SemaphoreType.DMA((2,)), pltpu.SemaphoreType.DMA(())])
    return pl.pallas_call(
        _expert_kernel,
        grid_spec=gs,
        out_shape=jax.ShapeDtypeStruct((n, d), F32),
        compiler_params=_cparams("arbitrary"),
        name="experts",
    )(elo, ehi, nrows, idx, idx, wlo, whi, h, w_gate, w_up, w_down, w_gate, w_up, w_down)


def _qkv_kernel(x_ref, y_ref, gate_ref, g_ref, sh_ref, sc_ref, w_ref, xo_ref, q_ref, k_ref, v_ref, *, qscale):
    x = x_ref[0] + gate_ref[0] * y_ref[...]
    xo_ref[0] = x
    h = _norm_mod(x, g_ref[...], sh_ref[0], sc_ref[0])
    r = _bdot(h, w_ref[...])
    d = x.shape[-1]
    npair = q_ref.shape[1]
    lanes = q_ref.shape[-1]
    for p in range(npair):
        q_ref[0, p] = (r[:, p * lanes:(p + 1) * lanes] * qscale).astype(BF16)
        k_ref[0, p] = r[:, d + p * lanes:d + (p + 1) * lanes].astype(BF16)
        v_ref[0, p] = r[:, 2 * d + p * lanes:2 * d + (p + 1) * lanes].astype(BF16)


def _flat_rows(row0, t, tm, d):
    return pl.BlockSpec((tm, d), lambda bi, i: (row0 // tm + bi * (t // tm) + i, 0))


def _qkv(x, y, row0, gate, g, shift, scale, w_qkv):
    b, t, d = x.shape
    tm = min(TOKEN_TILE, t)
    assert row0 % tm == 0
    npair = NA_HEADS // 2
    lanes = d // npair
    tok = pl.BlockSpec((1, tm, d), lambda bi, i: (bi, i, 0))
    row = pl.BlockSpec((1, 1, d), lambda bi, i: (bi, 0, 0))
    hp = pl.BlockSpec((1, npair, tm, lanes), lambda bi, i: (bi, 0, i, 0))
    hps = jax.ShapeDtypeStruct((b, npair, t, lanes), BF16)
    return pl.pallas_call(
        functools.partial(_qkv_kernel, qscale=float((d // NA_HEADS) ** -0.5)),
        grid=(b, t // tm),
        in_specs=[tok, _flat_rows(row0, t, tm, d), row, pl.BlockSpec((1, d), lambda bi, i: (0, 0)), row, row,
                  pl.BlockSpec(w_qkv.shape, lambda bi, i: (0, 0))],
        out_specs=[tok, hp, hp, hp],
        out_shape=[jax.ShapeDtypeStruct((b, t, d), F32), hps, hps, hps],
        compiler_params=_cparams("parallel", "parallel"),
        name="qkv",
    )(x, y, gate, g, shift, scale, w_qkv)


def _na_bias_tables(rpb):
    h = rpb.shape[0]
    col = jnp.arange(GRID_W)
    c_start = jnp.clip(col - NA_KC // 2, 0, GRID_W - NA_KC)
    col_ok = (col[None, :] >= c_start[:, None]) & (col[None, :] < c_start[:, None] + NA_KC)
    dc_idx = jnp.clip(col[None, :] - col[:, None] + NA_KC - 1, 0, 2 * NA_KC - 2)
    rc = jnp.where(col_ok[None, None], rpb[:, :, dc_idx].astype(F32), NEG_INF)
    dv = jnp.arange(NA_KR)[:, None]
    kr = jnp.arange(NA_KR)[None, :]
    t = rc[:, kr - dv + NA_KR - 1]
    t = t.reshape(h // 2, 2, NA_KR, NA_KR, GRID_W, GRID_W)
    return jnp.transpose(t, (0, 2, 1, 4, 3, 5)).reshape(h // 2, NA_KR, 2 * GRID_W, NA_KR * GRID_W)


def _attn_kernel(q_ref, kp_ref, kc_ref, kn_ref, vp_ref, vc_ref, vn_ref, kx_ref, vx_ref, bias_ref, o_ref,
                 kbuf, vbuf, q2_ref, sc_ref, pl_ref, pc_ref, li_ref, *, rows):
    i = pl.program_id(2)
    blk = kc_ref.shape[2]
    kbuf[0:blk, :] = kp_ref[0, 0]
    kbuf[blk:2 * blk, :] = kc_ref[0, 0]
    kbuf[2 * blk:3 * blk, :] = kn_ref[0, 0]
    vbuf[0:blk, :] = vp_ref[0, 0]
    vbuf[blk:2 * blk, :] = vc_ref[0, 0]
    vbuf[2 * blk:3 * blk, :] = vn_ref[0, 0]
    lanes = q_ref.shape[-1]
    w2 = 2 * GRID_W
    lane = lax.broadcasted_iota(jnp.int32, (GRID_W, lanes), 1)
    first = lane < lanes // 2
    nt_dims = (((1,), (1,)), ((), ()))
    for rho in range(NA_KR):
        q = q_ref[0, 0, rho * GRID_W:(rho + 1) * GRID_W, :]
        q2_ref[rho * w2:rho * w2 + GRID_W, :] = jnp.where(first, q, jnp.zeros_like(q))
        q2_ref[rho * w2 + GRID_W:(rho + 1) * w2, :] = jnp.where(first, jnp.zeros_like(q), q)
    sc_ref[...] = lax.dot_general(q2_ref[...], kx_ref[0, 0], nt_dims, preferred_element_type=F32)

    def window(rho):
        r = NA_KR * i + rho
        rs = jnp.clip(r - NA_KR // 2, 0, rows - NA_KR)
        return pl.multiple_of((rs - NA_KR * i + NA_KR) * GRID_W, GRID_W), r - rs

    for rho in range(NA_KR):
        wstart, dvar = window(rho)
        sl = slice(rho * w2, (rho + 1) * w2)
        s_loc = lax.dot_general(q2_ref[sl, :], kbuf[pl.ds(wstart, blk), :], nt_dims,
                                preferred_element_type=F32) + bias_ref[0, dvar]
        s_ctx = sc_ref[sl, :]
        m = jnp.maximum(jnp.max(s_loc, axis=-1, keepdims=True), jnp.max(s_ctx, axis=-1, keepdims=True))
        p_loc = jnp.exp(s_loc - m)
        p_ctx = jnp.exp(s_ctx - m)
        den = jnp.sum(p_loc, axis=-1, keepdims=True) + jnp.sum(p_ctx, axis=-1, keepdims=True)
        pl_ref[rho] = p_loc.astype(BF16)
        pc_ref[sl, :] = p_ctx.astype(BF16)
        li_ref[sl, :] = jnp.broadcast_to(1.0 / den, (w2, lanes))

    o_ctx = jnp.dot(pc_ref[...], vx_ref[0, 0], preferred_element_type=F32)
    for rho in range(NA_KR):
        wstart, _ = window(rho)
        sl = slice(rho * w2, (rho + 1) * w2)
        o = jnp.dot(pl_ref[rho], vbuf[pl.ds(wstart, blk), :], preferred_element_type=F32)
        o = (o + o_ctx[sl, :]) * li_ref[sl, :]
        o_ref[0, rho * GRID_W:(rho + 1) * GRID_W, :] = jnp.where(first, o[:GRID_W], o[GRID_W:]).astype(o_ref.dtype)


def _attention(q, k, v, kx, vx, bias):
    b, npair, t, lanes = q.shape
    c = kx.shape[2]
    blk = NA_KR * GRID_W
    nb = t // blk
    rows = t // GRID_W
    cur = lambda p, bi, i: (bi, p, i, 0)
    prv = lambda p, bi, i: (bi, p, jnp.maximum(i - 1, 0), 0)
    nxt = lambda p, bi, i: (bi, p, jnp.minimum(i + 1, nb - 1), 0)
    tb = lambda m: pl.BlockSpec((1, 1, blk, lanes), m)
    cx = pl.BlockSpec((1, 1, c, lanes), lambda p, bi, i: (bi, p, 0, 0))
    return pl.pallas_call(
        functools.partial(_attn_kernel, rows=rows),
        grid=(npair, b, nb),
        in_specs=[tb(cur), tb(prv), tb(cur), tb(nxt), tb(prv), tb(cur), tb(nxt), cx, cx,
                  pl.BlockSpec((1,) + bias.shape[1:], lambda p, bi, i: (p, 0, 0, 0))],
        out_specs=pl.BlockSpec((1, blk, lanes), lambda p, bi, i: (bi, i, p)),
        out_shape=jax.ShapeDtypeStruct((b, t, npair * lanes), BF16),
        scratch_shapes=[pltpu.VMEM((3 * blk, lanes), BF16), pltpu.VMEM((3 * blk, lanes), BF16),
                        pltpu.VMEM((2 * blk, lanes), BF16), pltpu.VMEM((2 * blk, c), F32),
                        pltpu.VMEM((NA_KR, 2 * GRID_W, blk), BF16), pltpu.VMEM((2 * blk, c), BF16),
                        pltpu.VMEM((2 * blk, lanes), F32)],
        compiler_params=_cparams("parallel", "parallel", "parallel"),
        name="na_attention",
    )(q, k, k, k, v, v, v, kx, vx, bias)


def _oproj_kernel(x_ref, o_ref, wo_ref, gate_ref, g2_ref, sh_ref, sc_ref, wr_ref, br_ref, xo_ref, h_ref, rec_ref):
    x = x_ref[0] + gate_ref[0] * jnp.dot(o_ref[0], wo_ref[...], preferred_element_type=F32)
    xo_ref[0] = x
    h = _norm_mod(x, g2_ref[...], sh_ref[0], sc_ref[0])
    h_ref[0] = h
    _route(h, wr_ref, br_ref, rec_ref)


def _oproj(x, o, w_out, gate, g2, shift, scale, wr, br):
    b, t, d = x.shape
    tm = min(TOKEN_TILE, t)
    tok = lambda n: pl.BlockSpec((1, tm, n), lambda bi, i: (bi, i, 0))
    row = pl.BlockSpec((1, 1, d), lambda bi, i: (bi, 0, 0))
    full = lambda shape: pl.BlockSpec(shape, lambda bi, i: (0,) * len(shape))
    return pl.pallas_call(
        _oproj_kernel,
        grid=(b, t // tm),
        in_specs=[tok(d), tok(d), full((d, d)), row, full((1, d)), row, row, full(wr.shape), full(br.shape)],
        out_specs=[tok(d), tok(d), tok(ROUTE_LANES)],
        out_shape=[jax.ShapeDtypeStruct((b, t, d), F32), jax.ShapeDtypeStruct((b, t, d), F32),
                   jax.ShapeDtypeStruct((b, t, ROUTE_LANES), F32)],
        compiler_params=_cparams("parallel", "parallel"),
        name="oproj_route",
    )(x, o, w_out, gate, g2, shift, scale, wr, br)


def _final_kernel(x_ref, y_ref, gate_ref, g_ref, o_ref):
    x = x_ref[0] + gate_ref[0] * y_ref[...]
    ms = jnp.mean(x * x, axis=-1, keepdims=True)
    o_ref[0] = (x * lax.rsqrt(ms + RMS_EPS)) * g_ref[...]


def _final(x, y, gate, g):
    b, t, d = x.shape
    tm = min(TOKEN_TILE, t)
    tok = pl.BlockSpec((1, tm, d), lambda bi, i: (bi, i, 0))
    return pl.pallas_call(
        _final_kernel,
        grid=(b, t // tm),
        in_specs=[tok, _flat_rows(0, t, tm, d), pl.BlockSpec((1, 1, d), lambda bi, i: (bi, 0, 0)),
                  pl.BlockSpec((1, d), lambda bi, i: (0, 0))],
        out_specs=tok,
        out_shape=jax.ShapeDtypeStruct((b, t, d), F32),
        compiler_params=_cparams("parallel", "parallel"),
        name="final_norm",
    )(x, y, gate, g)


def kernel(x, c, ctx, c_ctx, ada_w, ada_b, norm1_g, norm2_g, rec_w_in, rec_conv_w, rec_conv_b, lru_wa, lru_ba, lru_wx, lru_bx, lru_lambda, s5_a_re, s5_a_im, s5_log_dt, s5_b_re, s5_b_im, s5_c_re, s5_c_im, s5_d, s5_glu_w, s5_glu_b, rec_w_out, na_w_qkv, na_w_out, na_rpb, moe_r1_w, moe_r1_b, moe_r2_w, moe_r2_b, moe_w_gate, moe_w_up, moe_w_down, final_norm_g):
    b, t, d = x.shape
    tc = ctx.shape[1]
    assert ada_w.shape[0] == 2, "layer 0 recurrent mixer, layer 1 neighbourhood attention"
    w = rec_w_in.shape[-1] // 3

    rpad = -(b + 1) % 8
    cc = jnp.concatenate([c, c_ctx[None], jnp.zeros((rpad, d), F32)], axis=0)
    mod = _ada_mod(cc, ada_w, ada_b)

    def mods(layer, ctx_rows):
        rows = jnp.broadcast_to(mod[layer, b:b + 1], (b, 6 * d)) if ctx_rows else mod[layer, :b]
        return [rows[:, j * d:(j + 1) * d].reshape(b, 1, d) for j in range(6)]

    row = lambda v: v.reshape(1, -1)

    w_in = rec_w_in[0].astype(BF16)
    lru = [(_block_diag(lru_wa[0, dr]).astype(BF16), row(lru_ba[0, dr]), _block_diag(lru_wx[0, dr]).astype(BF16),
            row(lru_bx[0, dr]), row(lru_lambda[0, dr])) for dr in (0, 1)]
    s5t = _s5_tables(s5_a_re[0], s5_a_im[0], s5_log_dt[0], s5_b_re[0], s5_b_im[0], s5_c_re[0], s5_c_im[0])
    glu_w = s5_glu_w[0].astype(BF16)
    w_out0 = rec_w_out[0].astype(BF16)
    wr0, br0 = _router_tables(moe_r1_w[0], moe_r1_b[0], moe_r2_w[0], moe_r2_b[0])

    def mixer0(xs, m, h0_lru, h0_s5):
        xa, ga, ub = _inproj(xs, row(norm1_g[0]), m[0], m[1], w_in)
        hf = _lru_dir(xa, rec_conv_w[0], row(rec_conv_b[0]), *lru[0], h0_lru[0], False)
        hb = _lru_dir(xa, rec_conv_w[0], row(rec_conv_b[0]), *lru[1], h0_lru[1], True)
        ys, s5_fin = _s5(ub, s5t, h0_s5)
        x_mid, h2, rec = _merge(xs, ga, hf, hb, ub, ys, row(s5_d[0]), glu_w, row(s5_glu_b[0]), w_out0, m[2],
                                row(norm2_g[0]), m[3], m[4], wr0, br0)
        return x_mid, h2, rec, (hf[:, -1:], hb[:, :1]), s5_fin

    zl = jnp.zeros((b, 1, w), F32)
    zs = jnp.zeros((s5t[0].shape[0], b, s5t[0].shape[-1]), F32)
    mc0, ml0 = mods(0, True), mods(0, False)
    xc_mid, hc2, recc, lru_fin, s5_fin = mixer0(ctx, mc0, (zl, zl), zs)
    xl_mid, hl2, recl, _, _ = mixer0(x, ml0, lru_fin, s5_fin)

    h_all = jnp.concatenate([hl2.reshape(b * t, d), hc2.reshape(b * tc, d)], axis=0)
    rec_all = jnp.concatenate([recl.reshape(b * t, -1), recc.reshape(b * tc, -1)], axis=0)
    y_all = _experts(h_all, rec_all, moe_w_gate[0].astype(BF16), moe_w_up[0].astype(BF16),
                     moe_w_down[0].astype(BF16))

    w_qkv = na_w_qkv[0].astype(BF16)
    mc1, ml1 = mods(1, True), mods(1, False)
    _, _, kx, vx = _qkv(xc_mid, y_all, b * t, mc0[5], row(norm1_g[1]), mc1[0], mc1[1], w_qkv)
    x1, q, k, v = _qkv(xl_mid, y_all, 0, ml0[5], row(norm1_g[1]), ml1[0], ml1[1], w_qkv)
    o = _attention(q, k, v, kx, vx, _na_bias_tables(na_rpb[0]))
    wr1, br1 = _router_tables(moe_r1_w[1], moe_r1_b[1], moe_r2_w[1], moe_r2_b[1])
    x1_mid, h2, rec = _oproj(x1, o, na_w_out[0].astype(BF16), ml1[2], row(norm2_g[1]), ml1[3], ml1[4], wr1, br1)
    y1 = _experts(h2.reshape(b * t, d), rec.reshape(b * t, -1), moe_w_gate[1].astype(BF16),
                  moe_w_up[1].astype(BF16), moe_w_down[1].astype(BF16))
    return _final(x1_mid, y1, ml1[5], row(final_norm_g))
```

```python
import functools

import jax
import jax.numpy as jnp
from jax import lax
from jax.experimental import pallas as pl
from jax.experimental.pallas import tpu as pltpu

F32 = jnp.float32
BF16 = jnp.bfloat16
HIGHEST = lax.Precision.HIGHEST

RMS_EPS = 1e-6
GRID_W = 64
LRU_HEADS = 8
LRU_C = 8.0
S5_GROUP = 16
S5_CHUNK = 16
NA_HEADS = 16
NA_KR = 8
NA_KC = 16
NEG_INF = -1e30
MOE_GROUPS = 4
MOE_PER_GROUP = 8
MOE_EXPERTS = MOE_GROUPS * MOE_PER_GROUP
MOE_PAIRS = MOE_GROUPS * (MOE_PER_GROUP * (MOE_PER_GROUP - 1) // 2)
EXPERT_TILE = 256
ROUTE_LANES = 128
TOKEN_TILE = 512
VMEM_LIMIT = 56 * 1024 * 1024


def _cparams(*sem):
    return pltpu.CompilerParams(dimension_semantics=sem, vmem_limit_bytes=VMEM_LIMIT)


def _norm_mod(x, g, shift, scale):
    ms = jnp.mean(x * x, axis=-1, keepdims=True)
    return (x * lax.rsqrt(ms + RMS_EPS)) * g * (1.0 + scale) + shift


def _silu(x):
    return x * jax.nn.sigmoid(x)


def _gelu(x):
    return jax.nn.gelu(x, approximate=True)


def _bdot(a, b):
    return jnp.dot(a.astype(BF16), b, preferred_element_type=F32)


def _ada_kernel(c_ref, w_ref, b_ref, o_ref):
    o_ref[0] = jnp.dot(_silu(c_ref[...]), w_ref[0], preferred_element_type=F32,
                       precision=HIGHEST) + b_ref[0]


def _ada_mod(cc, ada_w, ada_b):
    n_layers, d, d6 = ada_w.shape
    r = cc.shape[0]
    return pl.pallas_call(
        _ada_kernel,
        grid=(n_layers, d6 // d),
        in_specs=[pl.BlockSpec((r, d), lambda l, j: (0, 0)),
                  pl.BlockSpec((1, d, d), lambda l, j: (l, 0, j)),
                  pl.BlockSpec((1, 1, d), lambda l, j: (l, 0, j))],
        out_specs=pl.BlockSpec((1, r, d), lambda l, j: (l, 0, j)),
        out_shape=jax.ShapeDtypeStruct((n_layers, r, d6), F32),
        compiler_params=_cparams("arbitrary", "arbitrary"),
        name="ada_mod",
    )(cc, ada_w, ada_b.reshape(n_layers, 1, d6))


def _inproj_kernel(x_ref, g_ref, sh_ref, sc_ref, w_ref, xa_ref, ga_ref, ub_ref):
    h = _norm_mod(x_ref[0], g_ref[...], sh_ref[0], sc_ref[0])
    r = _bdot(h, w_ref[...])
    w = xa_ref.shape[-1]
    xa_ref[0] = r[:, :w]
    ga_ref[0] = r[:, w:2 * w]
    ub_ref[0] = r[:, 2 * w:]


def _inproj(x, g, shift, scale, w_in):
    b, t, d = x.shape
    w = w_in.shape[1] // 3
    tm = min(TOKEN_TILE, t)
    row = pl.BlockSpec((1, 1, d), lambda bi, i: (bi, 0, 0))
    out = pl.BlockSpec((1, tm, w), lambda bi, i: (bi, i, 0))
    return pl.pallas_call(
        _inproj_kernel,
        grid=(b, t // tm),
        in_specs=[pl.BlockSpec((1, tm, d), lambda bi, i: (bi, i, 0)),
                  pl.BlockSpec((1, d), lambda bi, i: (0, 0)), row, row,
                  pl.BlockSpec(w_in.shape, lambda bi, i: (0, 0))],
        out_specs=[out, out, out],
        out_shape=[jax.ShapeDtypeStruct((b, t, w), F32)] * 3,
        compiler_params=_cparams("parallel", "parallel"),
        name="inproj",
    )(x, g, shift, scale, w_in)


def _lru_kernel(xc_ref, xp_ref, xn_ref, cw_ref, cb_ref, wa_ref, ba_ref, wx_ref, bx_ref, lam_ref,
                h0_ref, o_ref, ext_ref, a_ref, b_ref, car_ref, *, reverse, nt, tt):
    i = pl.program_id(1)
    ti = (nt - 1 - i) if reverse else i
    w = o_ref.shape[-1]

    @pl.when(i == 0)
    def _():
        car_ref[...] = h0_ref[0]

    ext_ref[0:8, :] = jnp.where(ti == 0, 0.0, xp_ref[0])
    ext_ref[8:8 + tt, :] = xc_ref[0]
    ext_ref[8 + tt:16 + tt, :] = jnp.where(ti == nt - 1, 0.0, xn_ref[0])
    cw = cw_ref[...]
    u = (ext_ref[6:6 + tt, :] * cw[0:1] + ext_ref[7:7 + tt, :] * cw[1:2]
         + ext_ref[8:8 + tt, :] * cw[2:3] + ext_ref[9:9 + tt, :] * cw[3:4]) + cb_ref[...]
    r = jax.nn.sigmoid(_bdot(u, wa_ref[...]) + ba_ref[...])
    ig = jax.nn.sigmoid(_bdot(u, wx_ref[...]) + bx_ref[...])
    log_a = (-LRU_C) * r * jax.nn.softplus(-lam_ref[...])
    a = jnp.exp(log_a)
    a_ref[...] = a
    b_ref[...] = jnp.sqrt(-jnp.tanh(log_a) * (1.0 + a * a)) * (ig * u)

    nsl = tt // 8
    row = lax.broadcasted_iota(jnp.int32, (8, w), 0)

    def slab(s, carry):
        off = pl.multiple_of(((nsl - 1 - s) if reverse else s) * 8, 8)
        a = a_ref[pl.ds(off, 8), :]
        bb = b_ref[pl.ds(off, 8), :]
        for k in (1, 2, 4):
            valid = (row < 8 - k) if reverse else (row >= k)
            sh = (8 - k) if reverse else k
            a_s = jnp.where(valid, pltpu.roll(a, sh, 0), 1.0)
            b_s = jnp.where(valid, pltpu.roll(bb, sh, 0), 0.0)
            bb = bb + a * b_s
            a = a * a_s
        h = bb + a * carry
        o_ref[0, pl.ds(off, 8), :] = h
        return h[0:1] if reverse else h[7:8]

    car_ref[...] = lax.fori_loop(0, nsl, slab, car_ref[...])


def _lru_dir(xa, conv_w, conv_b, wa_bd, ba, wx_bd, bx, lam, h0, reverse):
    b, t, w = xa.shape
    tt = min(TOKEN_TILE, t)
    nt = t // tt
    hb = tt // 8
    tile = (lambda i: nt - 1 - i) if reverse else (lambda i: i)
    full = lambda shape: pl.BlockSpec(shape, lambda bi, i: (0,) * len(shape))
    return pl.pallas_call(
        functools.partial(_lru_kernel, reverse=reverse, nt=nt, tt=tt),
        grid=(b, nt),
        in_specs=[pl.BlockSpec((1, tt, w), lambda bi, i: (bi, tile(i), 0)),
                  pl.BlockSpec((1, 8, w), lambda bi, i: (bi, jnp.maximum(tile(i) * hb - 1, 0), 0)),
                  pl.BlockSpec((1, 8, w), lambda bi, i: (bi, jnp.minimum((tile(i) + 1) * hb, t // 8 - 1), 0)),
                  full(conv_w.shape), full((1, w)), full((w, w)), full((1, w)), full((w, w)),
                  full((1, w)), full((1, w)),
                  pl.BlockSpec((1, 1, w), lambda bi, i: (bi, 0, 0))],
        out_specs=pl.BlockSpec((1, tt, w), lambda bi, i: (bi, tile(i), 0)),
        out_shape=jax.ShapeDtypeStruct((b, t, w), F32),
        scratch_shapes=[pltpu.VMEM((tt + 16, w), F32), pltpu.VMEM((tt, w), F32),
                        pltpu.VMEM((tt, w), F32), pltpu.VMEM((1, w), F32)],
        compiler_params=_cparams("parallel", "arbitrary"),
        name="lru_bwd" if reverse else "lru_fwd",
    )(xa, xa, xa, conv_w, conv_b, wa_bd, ba, wx_bd, bx, lam, h0)


def _block_diag(w):
    h, d, _ = w.shape
    eye = jnp.eye(h, dtype=w.dtype)
    return (eye[:, None, :, None] * w[:, :, None, :]).reshape(h * d, h * d)


def _s5_tables(a_re, a_im, log_dt, b_re, b_im, c_re, c_im):
    L = S5_CHUNK
    g, n = a_re.shape[1], a_re.shape[2]
    p = b_re.shape[-1]
    f = lambda x: x.astype(F32)
    a_re, a_im, b_re, b_im, c_re, c_im = map(f, (a_re, a_im, b_re, b_im, c_re, c_im))
    dt = jnp.exp(f(log_dt))[..., None]
    mag = jnp.exp(a_re * dt)
    lb_re, lb_im = mag * jnp.cos(a_im * dt), mag * jnp.sin(a_im * dt)
    den = a_re * a_re + a_im * a_im
    q_re = ((lb_re - 1.0) * a_re + lb_im * a_im) / den
    q_im = (lb_im * a_re - (lb_re - 1.0) * a_im) / den
    bb_re = q_re[..., None] * b_re - q_im[..., None] * b_im
    bb_im = q_re[..., None] * b_im + q_im[..., None] * b_re
    pw_re, pw_im = [jnp.ones_like(lb_re)], [jnp.zeros_like(lb_im)]
    for _ in range(L):
        r_, i_ = pw_re[-1], pw_im[-1]
        pw_re.append(r_ * lb_re - i_ * lb_im)
        pw_im.append(r_ * lb_im + i_ * lb_re)
    pw_re, pw_im = jnp.stack(pw_re, 1), jnp.stack(pw_im, 1)
    es = functools.partial(jnp.einsum, precision=HIGHEST)
    kf = jnp.arange(L - 1, -1, -1)
    kb = jnp.arange(L)
    inc = []
    for d_, ks in ((0, kf), (1, kb)):
        pr, pi = pw_re[d_][ks], pw_im[d_][ks]
        inc.append((pr[..., None] * bb_re[d_] - pi[..., None] * bb_im[d_],
                    pr[..., None] * bb_im[d_] + pi[..., None] * bb_re[d_]))
    gq = 128 // p
    nq = g // gq

    def quad_block_diag(a, outer, rpg, cpg):
        cols = a.shape[-1]
        full = jnp.broadcast_to(a[:, :, None], (nq, outer, gq, rpg, cols)).reshape(nq, outer * gq * rpg, cols)
        rg = (lax.broadcasted_iota(jnp.int32, full.shape, 1) // rpg) % gq
        cg = (lax.broadcasted_iota(jnp.int32, full.shape, 2) // cpg) % gq
        return jnp.where(rg == cg, full, 0.0).astype(BF16)

    inc_all = jnp.stack([jnp.stack(inc[0]), jnp.stack(inc[1])]).reshape(2, 2, L, nq, gq, n, p)
    wb = jnp.transpose(inc_all, (3, 2, 6, 0, 1, 4, 5)).reshape(nq, L, p, 4 * gq * n)
    wb = quad_block_diag(wb, L, p, n)
    rd = []
    for d_, ks in ((0, jnp.arange(1, L + 1)), (1, jnp.arange(L, 0, -1))):
        pr, pi = pw_re[d_][ks], pw_im[d_][ks]
        cl_re = c_re[d_][None] * pr[:, :, None, :] - c_im[d_][None] * pi[:, :, None, :]
        cl_im = c_re[d_][None] * pi[:, :, None, :] + c_im[d_][None] * pr[:, :, None, :]
        rd.append((cl_re, -cl_im))
    rd_all = jnp.stack([jnp.stack(rd[0]), jnp.stack(rd[1])]).reshape(2, 2, L, nq, gq, p, n)
    wc = jnp.transpose(rd_all, (3, 0, 1, 6, 2, 4, 5)).reshape(nq, 4, n, L * gq * p)
    wc = quad_block_diag(wc, 4, n, p)
    ker = []
    for d_ in (0, 1):
        pr, pi = pw_re[d_][:L], pw_im[d_][:L]
        cl_re = c_re[d_][None] * pr[:, :, None, :] - c_im[d_][None] * pi[:, :, None, :]
        cl_im = c_re[d_][None] * pi[:, :, None, :] + c_im[d_][None] * pr[:, :, None, :]
        ker.append(es('kgpn,gnq->kgpq', cl_re, bb_re[d_]) - es('kgpn,gnq->kgpq', cl_im, bb_im[d_]))
    s_i = jnp.arange(L)[:, None]
    t_i = jnp.arange(L)[None, :]
    kf_t = ker[0][jnp.clip(t_i - s_i, 0, L - 1)]
    kb_t = ker[1][jnp.clip(s_i - t_i, 0, L - 1)]
    m = (jnp.where((s_i <= t_i)[:, :, None, None, None], kf_t, 0.0)
         + jnp.where((s_i >= t_i)[:, :, None, None, None], kb_t, 0.0))
    mt = jnp.transpose(m.reshape(L, L, nq, gq, p, p), (2, 0, 5, 1, 3, 4)).reshape(nq, L, p, L * gq * p)
    mt = quad_block_diag(mt, L, p, p)
    l16 = jnp.stack([jnp.stack([pw_re[d_][L], pw_im[d_][L]]) for d_ in (0, 1)])
    l16 = l16.reshape(2, 2, nq, gq * n)
    return wb, mt, wc, l16


def _s5_inc_kernel(x_ref, wb_ref, xs_ref, s_ref):
    lanes = x_ref.shape[-1]
    for tau in range(S5_CHUNK):
        xs_ref[0, 0, :, tau * lanes:(tau + 1) * lanes] = x_ref[0, :, tau, :].astype(BF16)
    s_ref[0, 0] = jnp.dot(xs_ref[0, 0], wb_ref[0], preferred_element_type=F32)


def _s5_scan_kernel(sf_ref, sb_ref, l16_ref, h0_ref, hf_ref, hb_ref, hfin_ref, st_ref, *, jb):
    i = pl.program_id(0)
    nq = st_ref.shape[0]
    dl = sf_ref.shape[-1]
    w = dl // 2

    @pl.when(i == 0)
    def _():
        st_ref[...] = h0_ref[...]

    for jj in range(jb):
        for q in range(nq):
            for d, (src, dst, row) in enumerate(((sf_ref, hf_ref, jj), (sb_ref, hb_ref, jb - 1 - jj))):
                s = src[q, :, row, :]
                h = st_ref[q, :, d * dl:(d + 1) * dl]
                dst[q, :, row, :] = h
                lr, li = l16_ref[d, 0, q:q + 1, :], l16_ref[d, 1, q:q + 1, :]
                hr, hi = h[:, :w], h[:, w:]
                st_ref[q, :, d * dl:(d + 1) * dl] = jnp.concatenate(
                    [lr * hr - li * hi + s[:, :w], lr * hi + li * hr + s[:, w:]], axis=-1)

    @pl.when(i == pl.num_programs(0) - 1)
    def _():
        hfin_ref[...] = st_ref[...]


def _s5_out_kernel(xs_ref, hf_ref, hb_ref, mt_ref, wc_ref, y_ref):
    dl = hf_ref.shape[-1]
    y = (jnp.dot(xs_ref[0, 0], mt_ref[0], preferred_element_type=F32)
         + _bdot(hf_ref[0, 0], wc_ref[0, 0:dl, :]) + _bdot(hb_ref[0, 0], wc_ref[0, dl:2 * dl, :]))
    lanes = y_ref.shape[-1]
    for k in range(y_ref.shape[2]):
        y_ref[0, :, k, :] = y[:, k * lanes:(k + 1) * lanes]


def _s5(ub, tables, h0):
    wb, mt, wc, l16 = tables
    b, t, w = ub.shape
    L = S5_CHUNK
    nj = t // L
    nq = wb.shape[0]
    lanes = w // nq
    cl = L * lanes
    sl = wb.shape[-1]
    x4 = ub.reshape(b, nj, L, w)
    xs, s = pl.pallas_call(
        _s5_inc_kernel,
        grid=(nq, b),
        in_specs=[pl.BlockSpec((1, nj, L, lanes), lambda q, bi: (bi, 0, 0, q)),
                  pl.BlockSpec((1, cl, sl), lambda q, bi: (q, 0, 0))],
        out_specs=[pl.BlockSpec((1, 1, nj, cl), lambda q, bi: (q, bi, 0, 0)),
                   pl.BlockSpec((1, 1, nj, sl), lambda q, bi: (q, bi, 0, 0))],
        out_shape=[jax.ShapeDtypeStruct((nq, b, nj, cl), BF16), jax.ShapeDtypeStruct((nq, b, nj, sl), F32)],
        compiler_params=_cparams("parallel", "parallel"),
        name="s5_inc",
    )(x4, wb)
    jb = 8
    nblk = nj // jb
    dl = sl // 2
    hf, hb, hfin = pl.pallas_call(
        functools.partial(_s5_scan_kernel, jb=jb),
        grid=(nblk,),
        in_specs=[pl.BlockSpec((nq, b, jb, dl), lambda i: (0, 0, i, 0)),
                  pl.BlockSpec((nq, b, jb, dl), lambda i: (0, 0, nblk - 1 - i, 1)),
                  pl.BlockSpec(l16.shape, lambda i: (0, 0, 0, 0)),
                  pl.BlockSpec((nq, b, sl), lambda i: (0, 0, 0))],
        out_specs=[pl.BlockSpec((nq, b, jb, dl), lambda i: (0, 0, i, 0)),
                   pl.BlockSpec((nq, b, jb, dl), lambda i: (0, 0, nblk - 1 - i, 0)),
                   pl.BlockSpec((nq, b, sl), lambda i: (0, 0, 0))],
        out_shape=[jax.ShapeDtypeStruct((nq, b, nj, dl), F32), jax.ShapeDtypeStruct((nq, b, nj, dl), F32),
                   jax.ShapeDtypeStruct((nq, b, sl), F32)],
        scratch_shapes=[pltpu.VMEM((nq, b, sl), F32)],
        compiler_params=_cparams("arbitrary"),
        name="s5_scan",
    )(s, s, l16, h0)
    nh = 2
    y = pl.pallas_call(
        _s5_out_kernel,
        grid=(nq, nh, b),
        in_specs=[pl.BlockSpec((1, 1, nj, cl), lambda q, h, bi: (q, bi, 0, 0)),
                  pl.BlockSpec((1, 1, nj, dl), lambda q, h, bi: (q, bi, 0, 0)),
                  pl.BlockSpec((1, 1, nj, dl), lambda q, h, bi: (q, bi, 0, 0)),
                  pl.BlockSpec((1, cl, cl // nh), lambda q, h, bi: (q, 0, h)),
                  pl.BlockSpec((1, sl, cl // nh), lambda q, h, bi: (q, 0, h))],
        out_specs=pl.BlockSpec((1, nj, L // nh, lanes), lambda q, h, bi: (bi, 0, h, q)),
        out_shape=jax.ShapeDtypeStruct((b, nj, L, w), F32),
        compiler_params=_cparams("parallel", "parallel", "parallel"),
        name="s5_out",
    )(xs, hf, hb, mt, wc)
    return y.reshape(b, t, w), hfin


SUB_ROWS = 256


def _sub_rows(tm):
    return [slice(i, i + min(SUB_ROWS, tm)) for i in range(0, tm, min(SUB_ROWS, tm))]


def _route_logits(h, wr_ref, br_ref):
    h_hi = h.astype(BF16)
    h_lo = (h - h_hi.astype(F32)).astype(BF16)
    t = jnp.dot(h_hi, wr_ref[...], preferred_element_type=F32)
    return (t[:, :ROUTE_LANES] + t[:, ROUTE_LANES:]
            + jnp.dot(h_lo, wr_ref[:, 0:ROUTE_LANES], preferred_element_type=F32) + br_ref[...])


def _route_select(h, logits, hx_ref, rec_ref, rows):
    lane = lax.broadcasted_iota(jnp.int32, logits.shape, 1)
    big = jnp.int32(ROUTE_LANES)
    l1 = jnp.where(lane < MOE_GROUPS, logits, NEG_INF)
    m1 = jnp.max(l1, axis=-1, keepdims=True)
    gidx = jnp.min(jnp.where(l1 == m1, lane, big), axis=-1, keepdims=True)
    gval = 1.0 / jnp.sum(jnp.where(lane < MOE_GROUPS, jnp.exp(logits - m1), 0.0), axis=-1, keepdims=True)
    lo = MOE_GROUPS + MOE_PER_GROUP * gidx
    l2 = jnp.where((lane >= lo) & (lane < lo + MOE_PER_GROUP), logits, NEG_INF)
    v1 = jnp.max(l2, axis=-1, keepdims=True)
    i1 = jnp.min(jnp.where(l2 == v1, lane, big), axis=-1, keepdims=True)
    l2 = jnp.where(lane == i1, NEG_INF, l2)
    v2 = jnp.max(l2, axis=-1, keepdims=True)
    i2 = jnp.min(jnp.where(l2 == v2, lane, big), axis=-1, keepdims=True)
    e = jnp.exp(v2 - v1)
    wa = gval / (1.0 + e)
    wb = wa * e
    first = i1 <= i2
    w_lo, w_hi = jnp.where(first, wa, wb), jnp.where(first, wb, wa)
    e_lo = (jnp.minimum(i1, i2) - MOE_GROUPS).astype(F32)
    e_hi = (jnp.maximum(i1, i2) - MOE_GROUPS).astype(F32)
    rec = jnp.where(lane == 0, w_lo, jnp.where(lane == 1, w_hi, jnp.where(lane == 2, e_lo,
                    jnp.where(lane == 3, e_hi, 0.0))))
    d = h.shape[-1]
    hx_ref[0, rows, 0:d] = h
    hx_ref[0, rows, d:d + ROUTE_LANES] = rec
    rec_ref[0, rows, :] = rec


def _router_tables(r1_w, r1_b, r2_w, r2_b):
    d = r1_w.shape[0]
    wr = jnp.zeros((d, ROUTE_LANES), F32)
    wr = wr.at[:, :MOE_GROUPS].set(r1_w)
    wr = wr.at[:, MOE_GROUPS:MOE_GROUPS + MOE_EXPERTS].set(jnp.transpose(r2_w, (1, 0, 2)).reshape(d, MOE_EXPERTS))
    br = jnp.zeros((1, ROUTE_LANES), F32)
    br = br.at[0, :MOE_GROUPS].set(r1_b)
    br = br.at[0, MOE_GROUPS:MOE_GROUPS + MOE_EXPERTS].set(r2_b.reshape(MOE_EXPERTS))
    wr_hi = wr.astype(BF16)
    wr_lo = (wr - wr_hi.astype(F32)).astype(BF16)
    return jnp.concatenate([wr_hi, wr_lo], axis=1), br


def _merge_kernel(x_ref, ga_ref, hf_ref, hb_ref, ub_ref, ys_ref, d_ref, gw_ref, gb_ref, wo_ref, gate_ref,
                  g2_ref, sh_ref, sc_ref, wr_ref, br_ref, xo_ref, h_ref, rec_ref):
    w = ga_ref.shape[-1]
    subs = _sub_rows(x_ref.shape[1])
    y_a = [((hf_ref[0, r, :] + hb_ref[0, r, :]) * _gelu(ga_ref[0, r, :])).astype(BF16) for r in subs]
    y_s = [_gelu(ys_ref[0, r, :] + d_ref[...] * ub_ref[0, r, :]) for r in subs]
    glu = [_bdot(v, gw_ref[...]) for v in y_s]
    y_s = [(v * jax.nn.sigmoid(g + gb_ref[...])).astype(BF16) for v, g in zip(y_s, glu)]
    y = [jnp.dot(a, wo_ref[0:w, :], preferred_element_type=F32)
         + jnp.dot(s, wo_ref[w:2 * w, :], preferred_element_type=F32) for a, s in zip(y_a, y_s)]
    h = []
    for r, v in zip(subs, y):
        x = x_ref[0, r, :] + gate_ref[0] * v
        xo_ref[0, r, :] = x
        h.append(_norm_mod(x, g2_ref[...], sh_ref[0], sc_ref[0]))
    logits = [_route_logits(v, wr_ref, br_ref) for v in h]
    for r, v, lg in zip(subs, h, logits):
        _route_select(v, lg, h_ref, rec_ref, r)


def _merge(x, ga, hf, hb, ub, ys, s5_d, glu_w, glu_b, w_out, gate, g2, shift, scale, wr, br):
    b, t, d = x.shape
    w = ga.shape[-1]
    tm = min(TOKEN_TILE, t)
    tok = lambda n: pl.BlockSpec((1, tm, n), lambda bi, i: (bi, i, 0))
    row = pl.BlockSpec((1, 1, d), lambda bi, i: (bi, 0, 0))
    full = lambda shape: pl.BlockSpec(shape, lambda bi, i: (0,) * len(shape))
    return pl.pallas_call(
        _merge_kernel,
        grid=(b, t // tm),
        in_specs=[tok(d), tok(w), tok(w), tok(w), tok(w), tok(w), full((1, w)), full((w, w)), full((1, w)),
                  full((2 * w, d)), row, full((1, d)), row, row, full(wr.shape), full(br.shape)],
        out_specs=[tok(d), tok(d + ROUTE_LANES), tok(ROUTE_LANES)],
        out_shape=[jax.ShapeDtypeStruct((b, t, d), F32), jax.ShapeDtypeStruct((b, t, d + ROUTE_LANES), F32),
                   jax.ShapeDtypeStruct((b, t, ROUTE_LANES), F32)],
        compiler_params=_cparams("parallel", "parallel"),
        name="merge_route",
    )(x, ga, hf, hb, ub, ys, s5_d, glu_w, glu_b, w_out, gate, g2, shift, scale, wr, br)


DMA_GROUP = 8


EXPERT_CHUNK = 256


def _expert_kernel(elo_ref, ehi_ref, nrows_ref, gcur_ref, gnxt_ref, sprv_ref, scur_ref, hx_hbm,
                   gl_ref, ul_ref, dl_ref, gh_ref, uh_ref, dh_ref, y_hbm, xb0, xb1, yb0, yb1, gsem, ssem):
    t = pl.program_id(0)
    nt = pl.num_programs(0)
    n = nrows_ref[t]
    tm, d = yb0.shape
    f = gl_ref.shape[-1]
    xb, yb = (xb0, xb1), (yb0, yb1)

    def gather_row(iref, r, s):
        return pltpu.make_async_copy(hx_hbm.at[pl.ds(iref[0, 0, r], 1)], xb[s].at[pl.ds(r, 1)], gsem.at[s])

    def scatter_row(iref, r, s):
        return pltpu.make_async_copy(yb[s].at[pl.ds(r, 1)], y_hbm.at[pl.ds(iref[0, 0, r], 1)], ssem.at[s])

    def gather_wait(s):
        pltpu.make_async_copy(hx_hbm.at[pl.ds(0, tm)], xb[s], gsem.at[s]).wait()

    def scatter_wait(s):
        pltpu.make_async_copy(yb[s], y_hbm.at[pl.ds(0, tm)], ssem.at[s]).wait()

    def all_rows(fn):
        g = DMA_GROUP
        lax.fori_loop(0, tm // g, lambda k, c: ([fn(k * g + j) for j in range(g)], c)[1], 0)

    @pl.when(t == 0)
    def _():
        yb0[...] = jnp.zeros_like(yb0)
        yb1[...] = jnp.zeros_like(yb1)
        first_trash = pltpu.make_async_copy(yb0, y_hbm.at[pl.ds(y_hbm.shape[0] - 2 * tm, tm)], ssem.at[0])
        first_trash.start()
        first_trash.wait()
        all_rows(lambda r: gather_row(gcur_ref, r, 0).start())

    def step(s):
        o = 1 - s
        gather_wait(s)
        x = xb[s][:, 0:d].astype(BF16)
        gates = (xb[s][:, d:d + 1], xb[s][:, d + 1:d + 2])
        nchunk = f // EXPERT_CHUNK
        per = tm // (2 * nchunk)
        hids = []
        for e, (g_ref, u_ref) in enumerate(((gl_ref, ul_ref), (gh_ref, uh_ref))):
            for k in range(nchunk):
                c = e * nchunk + k
                for r in range(c * per, (c + 1) * per):
                    gather_row(gnxt_ref, r, o).start()
                    scatter_row(sprv_ref, r, o).start()
                cs = slice(k * EXPERT_CHUNK, (k + 1) * EXPERT_CHUNK)
                hid = _silu(jnp.dot(x, g_ref[0, :, cs], preferred_element_type=F32)) * jnp.dot(
                    x, u_ref[0, :, cs], preferred_element_type=F32)
                hids.append((hid * gates[e]).astype(BF16))
        y = None
        for e, d_ref in enumerate((dl_ref, dh_ref)):
            for k in range(nchunk):
                part = jnp.dot(hids[e * nchunk + k], d_ref[0, k * EXPERT_CHUNK:(k + 1) * EXPERT_CHUNK, :],
                               preferred_element_type=F32)
                y = part if y is None else y + part

        @pl.when(t > 0)
        def _():
            scatter_wait(s)
        yb[s][...] = y

        @pl.when(t == nt - 1)
        def _():
            gather_wait(o)
            scatter_wait(o)
            all_rows(lambda r: scatter_row(scur_ref, r, s).start())
            scatter_wait(s)

    def drain(s):
        o = 1 - s
        gather_wait(s)
        scatter_wait(s)
        all_rows(lambda r: scatter_row(sprv_ref, r, o).start())
        scatter_wait(o)

    had_rows = nrows_ref[jnp.maximum(t - 1, 0)] > 0
    for s in (0, 1):
        pl.when((t % 2 == s) & (n > 0))(functools.partial(step, s))
        pl.when((t % 2 == s) & (n == 0) & (t > 0) & had_rows)(functools.partial(drain, s))


def _moe_schedule(rec, tm):
    n = rec.shape[0]
    lo = jnp.clip(rec[:, 2].astype(jnp.int32), 0, MOE_EXPERTS - 1)
    hi = jnp.clip(rec[:, 3].astype(jnp.int32), 0, MOE_EXPERTS - 1)
    nbk = MOE_EXPERTS * MOE_EXPERTS
    bucket = lo * MOE_EXPERTS + hi
    order = jnp.argsort(bucket, stable=True).astype(jnp.int32)
    eids = jnp.arange(MOE_EXPERTS, dtype=jnp.int32)[None, :]
    count = jnp.einsum('nl,nh->lh', (lo[:, None] == eids).astype(F32), (hi[:, None] == eids).astype(F32),
                       precision=HIGHEST).astype(jnp.int32).reshape(nbk)
    start = jnp.cumsum(count) - count
    tiles = (count + tm - 1) // tm
    tile_end = jnp.cumsum(tiles)
    nt = n // tm + MOE_PAIRS
    tix = jnp.arange(nt, dtype=jnp.int32)
    total = tile_end[-1]
    bk = jnp.sum((tile_end[None, :] <= jnp.minimum(tix, total - 1)[:, None]).astype(jnp.int32), axis=1)
    bk = jnp.clip(bk, 0, nbk - 1)
    k = tix - (tile_end[bk] - tiles[bk])
    nrows = jnp.where(tix < total, jnp.clip(count[bk] - k * tm, 0, tm), 0).astype(jnp.int32)
    r = jnp.arange(tm, dtype=jnp.int32)[None, :]
    pos = jnp.clip(start[bk][:, None] + k[:, None] * tm + r, 0, n - 1)
    gidx = order[pos]
    sidx = jnp.where(r < nrows[:, None], gidx, n + (tix % 2)[:, None] * tm + r)
    sprev = jnp.concatenate([n + tm + r, sidx[:-1]], axis=0)
    shape = (nt, 1, tm)
    return (bk // MOE_EXPERTS, bk % MOE_EXPERTS, nrows, gidx.reshape(shape), sprev.reshape(shape),
            sidx.reshape(shape))


def _experts(hx, rec, w_gate, w_up, w_down):
    n, dx = hx.shape
    d = dx - ROUTE_LANES
    f = w_gate.shape[-1]
    tm = EXPERT_TILE
    elo, ehi, nrows, gidx, sprev, sidx = _moe_schedule(rec, tm)
    nt = nrows.shape[0]
    wspec = lambda shape, which: pl.BlockSpec(
        (1,) + shape, (lambda t, elo, ehi, nr: (elo[t], 0, 0)) if which == 0 else (lambda t, elo, ehi, nr: (ehi[t], 0, 0)))
    ispec = lambda m: pl.BlockSpec((1, 1, tm), m, memory_space=pltpu.SMEM)
    gs = pltpu.PrefetchScalarGridSpec(
        num_scalar_prefetch=3,
        grid=(nt,),
        in_specs=[ispec(lambda t, *_: (t, 0, 0)), ispec(lambda t, *_: (jnp.minimum(t + 1, nt - 1), 0, 0)),
                  ispec(lambda t, *_: (t, 0, 0)), ispec(lambda t, *_: (t, 0, 0)),
                  pl.BlockSpec(memory_space=pl.ANY),
                  wspec((d, f), 0), wspec((d, f), 0), wspec((f, d), 0),
                  wspec((d, f), 1), wspec((d, f), 1), wspec((f, d), 1)],
        out_specs=pl.BlockSpec(memory_space=pl.ANY),
        scratch_shapes=[pltpu.VMEM((tm, dx), F32), pltpu.VMEM((tm, dx), F32),
                        pltpu.VMEM((tm, d), F32), pltpu.VMEM((tm, d), F32),
                        pltpu.SemaphoreType.DMA((2,)), pltpu.SemaphoreType.DMA((2,))])
    return pl.pallas_call(
        _expert_kernel,
        grid_spec=gs,
        out_shape=jax.ShapeDtypeStruct((n + 2 * tm, d), F32),
        compiler_params=_cparams("arbitrary"),
        name="experts",
    )(elo, ehi, nrows, gidx, gidx, sprev, sidx, hx, w_gate, w_up, w_down, w_gate, w_up, w_down)


def _qkv_kernel(x_ref, y_ref, gate_ref, g_ref, sh_ref, sc_ref, w_ref, xo_ref, q_ref, k_ref, v_ref, *, qscale):
    x = x_ref[0] + gate_ref[0] * y_ref[...]
    xo_ref[0] = x
    h = _norm_mod(x, g_ref[...], sh_ref[0], sc_ref[0])
    r = _bdot(h, w_ref[...])
    d = x.shape[-1]
    npair = q_ref.shape[1]
    lanes = q_ref.shape[-1]
    for p in range(npair):
        q_ref[0, p] = (r[:, p * lanes:(p + 1) * lanes] * qscale).astype(BF16)
        k_ref[0, p] = r[:, d + p * lanes:d + (p + 1) * lanes].astype(BF16)
        v_ref[0, p] = r[:, 2 * d + p * lanes:2 * d + (p + 1) * lanes].astype(BF16)


def _flat_rows(row0, t, tm, d):
    return pl.BlockSpec((tm, d), lambda bi, i: (row0 // tm + bi * (t // tm) + i, 0))


def _qkv(x, y, row0, gate, g, shift, scale, w_qkv):
    b, t, d = x.shape
    tm = min(TOKEN_TILE, t)
    assert row0 % tm == 0
    npair = NA_HEADS // 2
    lanes = d // npair
    tok = pl.BlockSpec((1, tm, d), lambda bi, i: (bi, i, 0))
    row = pl.BlockSpec((1, 1, d), lambda bi, i: (bi, 0, 0))
    hp = pl.BlockSpec((1, npair, tm, lanes), lambda bi, i: (bi, 0, i, 0))
    hps = jax.ShapeDtypeStruct((b, npair, t, lanes), BF16)
    return pl.pallas_call(
        functools.partial(_qkv_kernel, qscale=float((d // NA_HEADS) ** -0.5)),
        grid=(b, t // tm),
        in_specs=[tok, _flat_rows(row0, t, tm, d), row, pl.BlockSpec((1, d), lambda bi, i: (0, 0)), row, row,
                  pl.BlockSpec(w_qkv.shape, lambda bi, i: (0, 0))],
        out_specs=[tok, hp, hp, hp],
        out_shape=[jax.ShapeDtypeStruct((b, t, d), F32), hps, hps, hps],
        compiler_params=_cparams("parallel", "parallel"),
        name="qkv",
    )(x, y, gate, g, shift, scale, w_qkv)


def _na_bias_tables(rpb):
    h = rpb.shape[0]
    col = jnp.arange(GRID_W)
    c_start = jnp.clip(col - NA_KC // 2, 0, GRID_W - NA_KC)
    col_ok = (col[None, :] >= c_start[:, None]) & (col[None, :] < c_start[:, None] + NA_KC)
    dc_idx = jnp.clip(col[None, :] - col[:, None] + NA_KC - 1, 0, 2 * NA_KC - 2)
    rc = jnp.where(col_ok[None, None], rpb[:, :, dc_idx].astype(F32), NEG_INF)
    dv = jnp.arange(NA_KR)[:, None]
    kr = jnp.arange(NA_KR)[None, :]
    t = rc[:, kr - dv + NA_KR - 1]
    t = t.reshape(h // 2, 2, NA_KR, NA_KR, GRID_W, GRID_W)
    return jnp.transpose(t, (0, 2, 1, 4, 3, 5)).reshape(h // 2, NA_KR, 2 * GRID_W, NA_KR * GRID_W)


def _attn_kernel(q_ref, kp_ref, kc_ref, kn_ref, vp_ref, vc_ref, vn_ref, kx_ref, vx_ref, bias_ref, o_ref,
                 kbuf, vbuf, q2_ref, sc_ref, pl_ref, pc_ref, li_ref, *, rows):
    i = pl.program_id(2)
    blk = kc_ref.shape[2]
    kbuf[0:blk, :] = kp_ref[0, 0]
    kbuf[blk:2 * blk, :] = kc_ref[0, 0]
    kbuf[2 * blk:3 * blk, :] = kn_ref[0, 0]
    vbuf[0:blk, :] = vp_ref[0, 0]
    vbuf[blk:2 * blk, :] = vc_ref[0, 0]
    vbuf[2 * blk:3 * blk, :] = vn_ref[0, 0]
    lanes = q_ref.shape[-1]
    w2 = 2 * GRID_W
    lane = lax.broadcasted_iota(jnp.int32, (GRID_W, lanes), 1)
    first = lane < lanes // 2
    nt_dims = (((1,), (1,)), ((), ()))
    for rho in range(NA_KR):
        q = q_ref[0, 0, rho * GRID_W:(rho + 1) * GRID_W, :]
        q2_ref[rho * w2:rho * w2 + GRID_W, :] = jnp.where(first, q, jnp.zeros_like(q))
        q2_ref[rho * w2 + GRID_W:(rho + 1) * w2, :] = jnp.where(first, jnp.zeros_like(q), q)
    sc_ref[...] = lax.dot_general(q2_ref[...], kx_ref[0, 0], nt_dims, preferred_element_type=F32)

    def window(rho):
        r = NA_KR * i + rho
        rs = jnp.clip(r - NA_KR // 2, 0, rows - NA_KR)
        return pl.multiple_of((rs - NA_KR * i + NA_KR) * GRID_W, GRID_W), r - rs

    for rho in range(NA_KR):
        wstart, dvar = window(rho)
        sl = slice(rho * w2, (rho + 1) * w2)
        s_loc = lax.dot_general(q2_ref[sl, :], kbuf[pl.ds(wstart, blk), :], nt_dims,
                                preferred_element_type=F32) + bias_ref[0, dvar]
        s_ctx = sc_ref[sl, :]
        m = jnp.maximum(jnp.max(s_loc, axis=-1, keepdims=True), jnp.max(s_ctx, axis=-1, keepdims=True))
        p_loc = jnp.exp(s_loc - m)
        p_ctx = jnp.exp(s_ctx - m)
        den = jnp.sum(p_loc, axis=-1, keepdims=True) + jnp.sum(p_ctx, axis=-1, keepdims=True)
        pl_ref[rho] = p_loc.astype(BF16)
        pc_ref[sl, :] = p_ctx.astype(BF16)
        li_ref[sl, :] = jnp.broadcast_to(1.0 / den, (w2, lanes))

    o_ctx = jnp.dot(pc_ref[...], vx_ref[0, 0], preferred_element_type=F32)
    for rho in range(NA_KR):
        wstart, _ = window(rho)
        sl = slice(rho * w2, (rho + 1) * w2)
        o = jnp.dot(pl_ref[rho], vbuf[pl.ds(wstart, blk), :], preferred_element_type=F32)
        o = (o + o_ctx[sl, :]) * li_ref[sl, :]
        o_ref[0, rho * GRID_W:(rho + 1) * GRID_W, :] = jnp.where(first, o[:GRID_W], o[GRID_W:]).astype(o_ref.dtype)


def _attention(q, k, v, kx, vx, bias):
    b, npair, t, lanes = q.shape
    c = kx.shape[2]
    blk = NA_KR * GRID_W
    nb = t // blk
    rows = t // GRID_W
    cur = lambda p, bi, i: (bi, p, i, 0)
    prv = lambda p, bi, i: (bi, p, jnp.maximum(i - 1, 0), 0)
    nxt = lambda p, bi, i: (bi, p, jnp.minimum(i + 1, nb - 1), 0)
    tb = lambda m: pl.BlockSpec((1, 1, blk, lanes), m)
    cx = pl.BlockSpec((1, 1, c, lanes), lambda p, bi, i: (bi, p, 0, 0))
    return pl.pallas_call(
        functools.partial(_attn_kernel, rows=rows),
        grid=(npair, b, nb),
        in_specs=[tb(cur), tb(prv), tb(cur), tb(nxt), tb(prv), tb(cur), tb(nxt), cx, cx,
                  pl.BlockSpec((1,) + bias.shape[1:], lambda p, bi, i: (p, 0, 0, 0))],
        out_specs=pl.BlockSpec((1, blk, lanes), lambda p, bi, i: (bi, i, p)),
        out_shape=jax.ShapeDtypeStruct((b, t, npair * lanes), BF16),
        scratch_shapes=[pltpu.VMEM((3 * blk, lanes), BF16), pltpu.VMEM((3 * blk, lanes), BF16),
                        pltpu.VMEM((2 * blk, lanes), BF16), pltpu.VMEM((2 * blk, c), F32),
                        pltpu.VMEM((NA_KR, 2 * GRID_W, blk), BF16), pltpu.VMEM((2 * blk, c), BF16),
                        pltpu.VMEM((2 * blk, lanes), F32)],
        compiler_params=_cparams("parallel", "parallel", "parallel"),
        name="na_attention",
    )(q, k, k, k, v, v, v, kx, vx, bias)


def _oproj_kernel(x_ref, o_ref, wo_ref, gate_ref, g2_ref, sh_ref, sc_ref, wr_ref, br_ref, xo_ref, h_ref, rec_ref):
    subs = _sub_rows(x_ref.shape[1])
    y = [jnp.dot(o_ref[0, r, :], wo_ref[...], preferred_element_type=F32) for r in subs]
    h = []
    for r, v in zip(subs, y):
        x = x_ref[0, r, :] + gate_ref[0] * v
        xo_ref[0, r, :] = x
        h.append(_norm_mod(x, g2_ref[...], sh_ref[0], sc_ref[0]))
    logits = [_route_logits(v, wr_ref, br_ref) for v in h]
    for r, v, lg in zip(subs, h, logits):
        _route_select(v, lg, h_ref, rec_ref, r)


def _oproj(x, o, w_out, gate, g2, shift, scale, wr, br):
    b, t, d = x.shape
    tm = min(TOKEN_TILE, t)
    tok = lambda n: pl.BlockSpec((1, tm, n), lambda bi, i: (bi, i, 0))
    row = pl.BlockSpec((1, 1, d), lambda bi, i: (bi, 0, 0))
    full = lambda shape: pl.BlockSpec(shape, lambda bi, i: (0,) * len(shape))
    return pl.pallas_call(
        _oproj_kernel,
        grid=(b, t // tm),
        in_specs=[tok(d), tok(d), full((d, d)), row, full((1, d)), row, row, full(wr.shape), full(br.shape)],
        out_specs=[tok(d), tok(d + ROUTE_LANES), tok(ROUTE_LANES)],
        out_shape=[jax.ShapeDtypeStruct((b, t, d), F32), jax.ShapeDtypeStruct((b, t, d + ROUTE_LANES), F32),
                   jax.ShapeDtypeStruct((b, t, ROUTE_LANES), F32)],
        compiler_params=_cparams("parallel", "parallel"),
        name="oproj_route",
    )(x, o, w_out, gate, g2, shift, scale, wr, br)


def _final_kernel(x_ref, y_ref, gate_ref, g_ref, o_ref):
    x = x_ref[0] + gate_ref[0] * y_ref[...]
    ms = jnp.mean(x * x, axis=-1, keepdims=True)
    o_ref[0] = (x * lax.rsqrt(ms + RMS_EPS)) * g_ref[...]


def _final(x, y, gate, g):
    b, t, d = x.shape
    tm = min(TOKEN_TILE, t)
    tok = pl.BlockSpec((1, tm, d), lambda bi, i: (bi, i, 0))
    return pl.pallas_call(
        _final_kernel,
        grid=(b, t // tm),
        in_specs=[tok, _flat_rows(0, t, tm, d), pl.BlockSpec((1, 1, d), lambda bi, i: (bi, 0, 0)),
                  pl.BlockSpec((1, d), lambda bi, i: (0, 0))],
        out_specs=tok,
        out_shape=jax.ShapeDtypeStruct((b, t, d), F32),
        compiler_params=_cparams("parallel", "parallel"),
        name="final_norm",
    )(x, y, gate, g)


def kernel(x, c, ctx, c_ctx, ada_w, ada_b, norm1_g, norm2_g, rec_w_in, rec_conv_w, rec_conv_b, lru_wa, lru_ba, lru_wx, lru_bx, lru_lambda, s5_a_re, s5_a_im, s5_log_dt, s5_b_re, s5_b_im, s5_c_re, s5_c_im, s5_d, s5_glu_w, s5_glu_b, rec_w_out, na_w_qkv, na_w_out, na_rpb, moe_r1_w, moe_r1_b, moe_r2_w, moe_r2_b, moe_w_gate, moe_w_up, moe_w_down, final_norm_g):
    b, t, d = x.shape
    tc = ctx.shape[1]
    assert ada_w.shape[0] == 2, "layer 0 recurrent mixer, layer 1 neighbourhood attention"
    w = rec_w_in.shape[-1] // 3

    rpad = -(b + 1) % 8
    cc = jnp.concatenate([c, c_ctx[None], jnp.zeros((rpad, d), F32)], axis=0)
    mod = _ada_mod(cc, ada_w, ada_b)

    def mods(layer, ctx_rows):
        rows = jnp.broadcast_to(mod[layer, b:b + 1], (b, 6 * d)) if ctx_rows else mod[layer, :b]
        return [rows[:, j * d:(j + 1) * d].reshape(b, 1, d) for j in range(6)]

    row = lambda v: v.reshape(1, -1)

    w_in = rec_w_in[0].astype(BF16)
    lru = [(_block_diag(lru_wa[0, dr]).astype(BF16), row(lru_ba[0, dr]), _block_diag(lru_wx[0, dr]).astype(BF16),
            row(lru_bx[0, dr]), row(lru_lambda[0, dr])) for dr in (0, 1)]
    s5t = _s5_tables(s5_a_re[0], s5_a_im[0], s5_log_dt[0], s5_b_re[0], s5_b_im[0], s5_c_re[0], s5_c_im[0])
    glu_w = s5_glu_w[0].astype(BF16)
    w_out0 = rec_w_out[0].astype(BF16)
    wr0, br0 = _router_tables(moe_r1_w[0], moe_r1_b[0], moe_r2_w[0], moe_r2_b[0])

    def mixer0(xs, m, h0_lru, h0_s5):
        xa, ga, ub = _inproj(xs, row(norm1_g[0]), m[0], m[1], w_in)
        hf = _lru_dir(xa, rec_conv_w[0], row(rec_conv_b[0]), *lru[0], h0_lru[0], False)
        hb = _lru_dir(xa, rec_conv_w[0], row(rec_conv_b[0]), *lru[1], h0_lru[1], True)
        ys, s5_fin = _s5(ub, s5t, h0_s5)
        x_mid, h2, rec = _merge(xs, ga, hf, hb, ub, ys, row(s5_d[0]), glu_w, row(s5_glu_b[0]), w_out0, m[2],
                                row(norm2_g[0]), m[3], m[4], wr0, br0)
        return x_mid, h2, rec, (hf[:, -1:], hb[:, :1]), s5_fin

    zl = jnp.zeros((b, 1, w), F32)
    zs = jnp.zeros((s5t[0].shape[0], b, s5t[0].shape[-1]), F32)
    mc0, ml0 = mods(0, True), mods(0, False)
    xc_mid, hc2, recc, lru_fin, s5_fin = mixer0(ctx, mc0, (zl, zl), zs)
    xl_mid, hl2, recl, _, _ = mixer0(x, ml0, lru_fin, s5_fin)

    h_all = jnp.concatenate([hl2.reshape(b * t, -1), hc2.reshape(b * tc, -1)], axis=0)
    rec_all = jnp.concatenate([recl.reshape(b * t, -1), recc.reshape(b * tc, -1)], axis=0)
    y_all = _experts(h_all, rec_all, moe_w_gate[0].astype(BF16), moe_w_up[0].astype(BF16),
                     moe_w_down[0].astype(BF16))

    w_qkv = na_w_qkv[0].astype(BF16)
    mc1, ml1 = mods(1, True), mods(1, False)
    _, _, kx, vx = _qkv(xc_mid, y_all, b * t, mc0[5], row(norm1_g[1]), mc1[0], mc1[1], w_qkv)
    x1, q, k, v = _qkv(xl_mid, y_all, 0, ml0[5], row(norm1_g[1]), ml1[0], ml1[1], w_qkv)
    o = _attention(q, k, v, kx, vx, _na_bias_tables(na_rpb[0]))
    wr1, br1 = _router_tables(moe_r1_w[1], moe_r1_b[1], moe_r2_w[1], moe_r2_b[1])
    x1_mid, h2, rec = _oproj(x1, o, na_w_out[0].astype(BF16), ml1[2], row(norm2_g[1]), ml1[3], ml1[4], wr1, br1)
    y1 = _experts(h2.reshape(b * t, -1), rec.reshape(b * t, -1), moe_w_gate[1].astype(BF16),
                  moe_w_up[1].astype(BF16), moe_w_down[1].astype(BF16))
    return _final(x1_mid, y1, ml1[5], row(final_norm_g))
```

```python
import functools

import jax
import jax.numpy as jnp
from jax import lax
from jax.experimental import pallas as pl
from jax.experimental.pallas import tpu as pltpu

F32 = jnp.float32
BF16 = jnp.bfloat16
HIGHEST = lax.Precision.HIGHEST

RMS_EPS = 1e-6
GRID_W = 64
LRU_HEADS = 8
LRU_C = 8.0
S5_GROUP = 16
S5_CHUNK = 16
NA_HEADS = 16
NA_KR = 8
NA_KC = 16
NEG_INF = -1e30
MOE_GROUPS = 4
MOE_PER_GROUP = 8
MOE_EXPERTS = MOE_GROUPS * MOE_PER_GROUP
MOE_PAIRS = MOE_GROUPS * (MOE_PER_GROUP * (MOE_PER_GROUP - 1) // 2)
EXPERT_TILE = 256
ROUTE_LANES = 128
TOKEN_TILE = 512
VMEM_LIMIT = 56 * 1024 * 1024


def _cparams(*sem):
    return pltpu.CompilerParams(dimension_semantics=sem, vmem_limit_bytes=VMEM_LIMIT)


def _norm_mod(x, g, shift, scale):
    ms = jnp.mean(x * x, axis=-1, keepdims=True)
    return (x * lax.rsqrt(ms + RMS_EPS)) * g * (1.0 + scale) + shift


def _silu(x):
    return x * jax.nn.sigmoid(x)


def _gelu(x):
    return jax.nn.gelu(x, approximate=True)


def _bdot(a, b):
    return jnp.dot(a.astype(BF16), b, preferred_element_type=F32)


def _ada_kernel(c_ref, w_ref, b_ref, o_ref):
    o_ref[0] = jnp.dot(_silu(c_ref[...]), w_ref[0], preferred_element_type=F32,
                       precision=HIGHEST) + b_ref[0]


def _ada_mod(cc, ada_w, ada_b):
    n_layers, d, d6 = ada_w.shape
    r = cc.shape[0]
    return pl.pallas_call(
        _ada_kernel,
        grid=(n_layers, d6 // d),
        in_specs=[pl.BlockSpec((r, d), lambda l, j: (0, 0)),
                  pl.BlockSpec((1, d, d), lambda l, j: (l, 0, j)),
                  pl.BlockSpec((1, 1, d), lambda l, j: (l, 0, j))],
        out_specs=pl.BlockSpec((1, r, d), lambda l, j: (l, 0, j)),
        out_shape=jax.ShapeDtypeStruct((n_layers, r, d6), F32),
        compiler_params=_cparams("arbitrary", "arbitrary"),
        name="ada_mod",
    )(cc, ada_w, ada_b.reshape(n_layers, 1, d6))


def _inproj_kernel(x_ref, g_ref, sh_ref, sc_ref, w_ref, xa_ref, ga_ref, ub_ref):
    h = _norm_mod(x_ref[0], g_ref[...], sh_ref[0], sc_ref[0])
    r = _bdot(h, w_ref[...])
    w = xa_ref.shape[-1]
    xa_ref[0] = r[:, :w]
    ga_ref[0] = r[:, w:2 * w]
    ub_ref[0] = r[:, 2 * w:]


def _inproj(x, g, shift, scale, w_in):
    b, t, d = x.shape
    w = w_in.shape[1] // 3
    tm = min(TOKEN_TILE, t)
    row = pl.BlockSpec((1, 1, d), lambda bi, i: (bi, 0, 0))
    out = pl.BlockSpec((1, tm, w), lambda bi, i: (bi, i, 0))
    return pl.pallas_call(
        _inproj_kernel,
        grid=(b, t // tm),
        in_specs=[pl.BlockSpec((1, tm, d), lambda bi, i: (bi, i, 0)),
                  pl.BlockSpec((1, d), lambda bi, i: (0, 0)), row, row,
                  pl.BlockSpec(w_in.shape, lambda bi, i: (0, 0))],
        out_specs=[out, out, out],
        out_shape=[jax.ShapeDtypeStruct((b, t, w), F32)] * 3,
        compiler_params=_cparams("parallel", "parallel"),
        name="inproj",
    )(x, g, shift, scale, w_in)


def _lru_kernel(xc_ref, xp_ref, xn_ref, cw_ref, cb_ref, wa_ref, ba_ref, wx_ref, bx_ref, lam_ref,
                h0_ref, o_ref, ext_ref, a_ref, b_ref, car_ref, *, reverse, nt, tt):
    i = pl.program_id(1)
    ti = (nt - 1 - i) if reverse else i
    w = o_ref.shape[-1]

    @pl.when(i == 0)
    def _():
        car_ref[...] = h0_ref[0]

    ext_ref[0:8, :] = jnp.where(ti == 0, 0.0, xp_ref[0])
    ext_ref[8:8 + tt, :] = xc_ref[0]
    ext_ref[8 + tt:16 + tt, :] = jnp.where(ti == nt - 1, 0.0, xn_ref[0])
    cw = cw_ref[...]
    u = (ext_ref[6:6 + tt, :] * cw[0:1] + ext_ref[7:7 + tt, :] * cw[1:2]
         + ext_ref[8:8 + tt, :] * cw[2:3] + ext_ref[9:9 + tt, :] * cw[3:4]) + cb_ref[...]
    r = jax.nn.sigmoid(_bdot(u, wa_ref[...]) + ba_ref[...])
    ig = jax.nn.sigmoid(_bdot(u, wx_ref[...]) + bx_ref[...])
    log_a = (-LRU_C) * r * jax.nn.softplus(-lam_ref[...])
    a = jnp.exp(log_a)
    a_ref[...] = a
    b_ref[...] = jnp.sqrt(-jnp.tanh(log_a) * (1.0 + a * a)) * (ig * u)

    nsl = tt // 8
    row = lax.broadcasted_iota(jnp.int32, (8, w), 0)

    def slab(s, carry):
        off = pl.multiple_of(((nsl - 1 - s) if reverse else s) * 8, 8)
        a = a_ref[pl.ds(off, 8), :]
        bb = b_ref[pl.ds(off, 8), :]
        for k in (1, 2, 4):
            valid = (row < 8 - k) if reverse else (row >= k)
            sh = (8 - k) if reverse else k
            a_s = jnp.where(valid, pltpu.roll(a, sh, 0), 1.0)
            b_s = jnp.where(valid, pltpu.roll(bb, sh, 0), 0.0)
            bb = bb + a * b_s
            a = a * a_s
        h = bb + a * carry
        o_ref[0, pl.ds(off, 8), :] = h
        return h[0:1] if reverse else h[7:8]

    car_ref[...] = lax.fori_loop(0, nsl, slab, car_ref[...])


def _lru_dir(xa, conv_w, conv_b, wa_bd, ba, wx_bd, bx, lam, h0, reverse):
    b, t, w = xa.shape
    tt = min(TOKEN_TILE, t)
    nt = t // tt
    hb = tt // 8
    tile = (lambda i: nt - 1 - i) if reverse else (lambda i: i)
    full = lambda shape: pl.BlockSpec(shape, lambda bi, i: (0,) * len(shape))
    return pl.pallas_call(
        functools.partial(_lru_kernel, reverse=reverse, nt=nt, tt=tt),
        grid=(b, nt),
        in_specs=[pl.BlockSpec((1, tt, w), lambda bi, i: (bi, tile(i), 0)),
                  pl.BlockSpec((1, 8, w), lambda bi, i: (bi, jnp.maximum(tile(i) * hb - 1, 0), 0)),
                  pl.BlockSpec((1, 8, w), lambda bi, i: (bi, jnp.minimum((tile(i) + 1) * hb, t // 8 - 1), 0)),
                  full(conv_w.shape), full((1, w)), full((w, w)), full((1, w)), full((w, w)),
                  full((1, w)), full((1, w)),
                  pl.BlockSpec((1, 1, w), lambda bi, i: (bi, 0, 0))],
        out_specs=pl.BlockSpec((1, tt, w), lambda bi, i: (bi, tile(i), 0)),
        out_shape=jax.ShapeDtypeStruct((b, t, w), F32),
        scratch_shapes=[pltpu.VMEM((tt + 16, w), F32), pltpu.VMEM((tt, w), F32),
                        pltpu.VMEM((tt, w), F32), pltpu.VMEM((1, w), F32)],
        compiler_params=_cparams("parallel", "arbitrary"),
        name="lru_bwd" if reverse else "lru_fwd",
    )(xa, xa, xa, conv_w, conv_b, wa_bd, ba, wx_bd, bx, lam, h0)


def _block_diag(w):
    h, d, _ = w.shape
    eye = jnp.eye(h, dtype=w.dtype)
    return (eye[:, None, :, None] * w[:, :, None, :]).reshape(h * d, h * d)


def _s5_tables(a_re, a_im, log_dt, b_re, b_im, c_re, c_im):
    L = S5_CHUNK
    g, n = a_re.shape[1], a_re.shape[2]
    p = b_re.shape[-1]
    f = lambda x: x.astype(F32)
    a_re, a_im, b_re, b_im, c_re, c_im = map(f, (a_re, a_im, b_re, b_im, c_re, c_im))
    dt = jnp.exp(f(log_dt))[..., None]
    mag = jnp.exp(a_re * dt)
    lb_re, lb_im = mag * jnp.cos(a_im * dt), mag * jnp.sin(a_im * dt)
    den = a_re * a_re + a_im * a_im
    q_re = ((lb_re - 1.0) * a_re + lb_im * a_im) / den
    q_im = (lb_im * a_re - (lb_re - 1.0) * a_im) / den
    bb_re = q_re[..., None] * b_re - q_im[..., None] * b_im
    bb_im = q_re[..., None] * b_im + q_im[..., None] * b_re
    pw_re, pw_im = [jnp.ones_like(lb_re)], [jnp.zeros_like(lb_im)]
    for _ in range(L):
        r_, i_ = pw_re[-1], pw_im[-1]
        pw_re.append(r_ * lb_re - i_ * lb_im)
        pw_im.append(r_ * lb_im + i_ * lb_re)
    pw_re, pw_im = jnp.stack(pw_re, 1), jnp.stack(pw_im, 1)
    es = functools.partial(jnp.einsum, precision=HIGHEST)
    kf = jnp.arange(L - 1, -1, -1)
    kb = jnp.arange(L)
    inc = []
    for d_, ks in ((0, kf), (1, kb)):
        pr, pi = pw_re[d_][ks], pw_im[d_][ks]
        inc.append((pr[..., None] * bb_re[d_] - pi[..., None] * bb_im[d_],
                    pr[..., None] * bb_im[d_] + pi[..., None] * bb_re[d_]))
    gq = 128 // p
    nq = g // gq

    def quad_block_diag(a, outer, rpg, cpg):
        cols = a.shape[-1]
        full = jnp.broadcast_to(a[:, :, None], (nq, outer, gq, rpg, cols)).reshape(nq, outer * gq * rpg, cols)
        rg = (lax.broadcasted_iota(jnp.int32, full.shape, 1) // rpg) % gq
        cg = (lax.broadcasted_iota(jnp.int32, full.shape, 2) // cpg) % gq
        return jnp.where(rg == cg, full, 0.0).astype(BF16)

    inc_all = jnp.stack([jnp.stack(inc[0]), jnp.stack(inc[1])]).reshape(2, 2, L, nq, gq, n, p)
    wb = jnp.transpose(inc_all, (3, 2, 6, 0, 1, 4, 5)).reshape(nq, L, p, 4 * gq * n)
    wb = quad_block_diag(wb, L, p, n)
    rd = []
    for d_, ks in ((0, jnp.arange(1, L + 1)), (1, jnp.arange(L, 0, -1))):
        pr, pi = pw_re[d_][ks], pw_im[d_][ks]
        cl_re = c_re[d_][None] * pr[:, :, None, :] - c_im[d_][None] * pi[:, :, None, :]
        cl_im = c_re[d_][None] * pi[:, :, None, :] + c_im[d_][None] * pr[:, :, None, :]
        rd.append((cl_re, -cl_im))
    rd_all = jnp.stack([jnp.stack(rd[0]), jnp.stack(rd[1])]).reshape(2, 2, L, nq, gq, p, n)
    wc = jnp.transpose(rd_all, (3, 0, 1, 6, 2, 4, 5)).reshape(nq, 4, n, L * gq * p)
    wc = quad_block_diag(wc, 4, n, p)
    ker = []
    for d_ in (0, 1):
        pr, pi = pw_re[d_][:L], pw_im[d_][:L]
        cl_re = c_re[d_][None] * pr[:, :, None, :] - c_im[d_][None] * pi[:, :, None, :]
        cl_im = c_re[d_][None] * pi[:, :, None, :] + c_im[d_][None] * pr[:, :, None, :]
        ker.append(es('kgpn,gnq->kgpq', cl_re, bb_re[d_]) - es('kgpn,gnq->kgpq', cl_im, bb_im[d_]))
    s_i = jnp.arange(L)[:, None]
    t_i = jnp.arange(L)[None, :]
    kf_t = ker[0][jnp.clip(t_i - s_i, 0, L - 1)]
    kb_t = ker[1][jnp.clip(s_i - t_i, 0, L - 1)]
    m = (jnp.where((s_i <= t_i)[:, :, None, None, None], kf_t, 0.0)
         + jnp.where((s_i >= t_i)[:, :, None, None, None], kb_t, 0.0))
    mt = jnp.transpose(m.reshape(L, L, nq, gq, p, p), (2, 0, 5, 1, 3, 4)).reshape(nq, L, p, L * gq * p)
    mt = quad_block_diag(mt, L, p, p)
    l16 = jnp.stack([jnp.stack([pw_re[d_][L], pw_im[d_][L]]) for d_ in (0, 1)])
    l16 = l16.reshape(2, 2, nq, gq * n)
    return wb, mt, wc, l16


def _s5_inc_kernel(x_ref, wb_ref, xs_ref, s_ref):
    lanes = x_ref.shape[-1]
    for tau in range(S5_CHUNK):
        xs_ref[0, 0, :, tau * lanes:(tau + 1) * lanes] = x_ref[0, :, tau, :].astype(BF16)
    s_ref[0, 0] = jnp.dot(xs_ref[0, 0], wb_ref[0], preferred_element_type=F32)


def _s5_scan_kernel(sf_ref, sb_ref, l16_ref, h0_ref, hf_ref, hb_ref, hfin_ref, st_ref, *, jb):
    i = pl.program_id(0)
    nq = st_ref.shape[0]
    dl = sf_ref.shape[-1]
    w = dl // 2

    @pl.when(i == 0)
    def _():
        st_ref[...] = h0_ref[...]

    for jj in range(jb):
        for q in range(nq):
            for d, (src, dst, row) in enumerate(((sf_ref, hf_ref, jj), (sb_ref, hb_ref, jb - 1 - jj))):
                s = src[q, :, row, :]
                h = st_ref[q, :, d * dl:(d + 1) * dl]
                dst[q, :, row, :] = h
                lr, li = l16_ref[d, 0, q:q + 1, :], l16_ref[d, 1, q:q + 1, :]
                hr, hi = h[:, :w], h[:, w:]
                st_ref[q, :, d * dl:(d + 1) * dl] = jnp.concatenate(
                    [lr * hr - li * hi + s[:, :w], lr * hi + li * hr + s[:, w:]], axis=-1)

    @pl.when(i == pl.num_programs(0) - 1)
    def _():
        hfin_ref[...] = st_ref[...]


def _s5_out_kernel(xs_ref, hf_ref, hb_ref, mt_ref, wc_ref, y_ref):
    dl = hf_ref.shape[-1]
    y = (jnp.dot(xs_ref[0, 0], mt_ref[0], preferred_element_type=F32)
         + _bdot(hf_ref[0, 0], wc_ref[0, 0:dl, :]) + _bdot(hb_ref[0, 0], wc_ref[0, dl:2 * dl, :]))
    lanes = y_ref.shape[-1]
    for k in range(y_ref.shape[2]):
        y_ref[0, :, k, :] = y[:, k * lanes:(k + 1) * lanes]


def _s5(ub, tables, h0):
    wb, mt, wc, l16 = tables
    b, t, w = ub.shape
    L = S5_CHUNK
    nj = t // L
    nq = wb.shape[0]
    lanes = w // nq
    cl = L * lanes
    sl = wb.shape[-1]
    x4 = ub.reshape(b, nj, L, w)
    xs, s = pl.pallas_call(
        _s5_inc_kernel,
        grid=(nq, b),
        in_specs=[pl.BlockSpec((1, nj, L, lanes), lambda q, bi: (bi, 0, 0, q)),
                  pl.BlockSpec((1, cl, sl), lambda q, bi: (q, 0, 0))],
        out_specs=[pl.BlockSpec((1, 1, nj, cl), lambda q, bi: (q, bi, 0, 0)),
                   pl.BlockSpec((1, 1, nj, sl), lambda q, bi: (q, bi, 0, 0))],
        out_shape=[jax.ShapeDtypeStruct((nq, b, nj, cl), BF16), jax.ShapeDtypeStruct((nq, b, nj, sl), F32)],
        compiler_params=_cparams("parallel", "parallel"),
        name="s5_inc",
    )(x4, wb)
    jb = 8
    nblk = nj // jb
    dl = sl // 2
    hf, hb, hfin = pl.pallas_call(
        functools.partial(_s5_scan_kernel, jb=jb),
        grid=(nblk,),
        in_specs=[pl.BlockSpec((nq, b, jb, dl), lambda i: (0, 0, i, 0)),
                  pl.BlockSpec((nq, b, jb, dl), lambda i: (0, 0, nblk - 1 - i, 1)),
                  pl.BlockSpec(l16.shape, lambda i: (0, 0, 0, 0)),
                  pl.BlockSpec((nq, b, sl), lambda i: (0, 0, 0))],
        out_specs=[pl.BlockSpec((nq, b, jb, dl), lambda i: (0, 0, i, 0)),
                   pl.BlockSpec((nq, b, jb, dl), lambda i: (0, 0, nblk - 1 - i, 0)),
                   pl.BlockSpec((nq, b, sl), lambda i: (0, 0, 0))],
        out_shape=[jax.ShapeDtypeStruct((nq, b, nj, dl), F32), jax.ShapeDtypeStruct((nq, b, nj, dl), F32),
                   jax.ShapeDtypeStruct((nq, b, sl), F32)],
        scratch_shapes=[pltpu.VMEM((nq, b, sl), F32)],
        compiler_params=_cparams("arbitrary"),
        name="s5_scan",
    )(s, s, l16, h0)
    nh = 2
    y = pl.pallas_call(
        _s5_out_kernel,
        grid=(nq, nh, b),
        in_specs=[pl.BlockSpec((1, 1, nj, cl), lambda q, h, bi: (q, bi, 0, 0)),
                  pl.BlockSpec((1, 1, nj, dl), lambda q, h, bi: (q, bi, 0, 0)),
                  pl.BlockSpec((1, 1, nj, dl), lambda q, h, bi: (q, bi, 0, 0)),
                  pl.BlockSpec((1, cl, cl // nh), lambda q, h, bi: (q, 0, h)),
                  pl.BlockSpec((1, sl, cl // nh), lambda q, h, bi: (q, 0, h))],
        out_specs=pl.BlockSpec((1, nj, L // nh, lanes), lambda q, h, bi: (bi, 0, h, q)),
        out_shape=jax.ShapeDtypeStruct((b, nj, L, w), F32),
        compiler_params=_cparams("parallel", "parallel", "parallel"),
        name="s5_out",
    )(xs, hf, hb, mt, wc)
    return y.reshape(b, t, w), hfin


SUB_ROWS = 256


def _sub_rows(tm):
    return [slice(i, i + min(SUB_ROWS, tm)) for i in range(0, tm, min(SUB_ROWS, tm))]


def _route_logits(h, wr_ref, br_ref):
    h_hi = h.astype(BF16)
    h_lo = (h - h_hi.astype(F32)).astype(BF16)
    t = jnp.dot(h_hi, wr_ref[...], preferred_element_type=F32)
    return (t[:, :ROUTE_LANES] + t[:, ROUTE_LANES:]
            + jnp.dot(h_lo, wr_ref[:, 0:ROUTE_LANES], preferred_element_type=F32) + br_ref[...])


def _route_select(h, logits, hx_ref, rec_ref, rows):
    lane = lax.broadcasted_iota(jnp.int32, logits.shape, 1)
    big = jnp.int32(ROUTE_LANES)
    l1 = jnp.where(lane < MOE_GROUPS, logits, NEG_INF)
    m1 = jnp.max(l1, axis=-1, keepdims=True)
    gidx = jnp.min(jnp.where(l1 == m1, lane, big), axis=-1, keepdims=True)
    gval = 1.0 / jnp.sum(jnp.where(lane < MOE_GROUPS, jnp.exp(logits - m1), 0.0), axis=-1, keepdims=True)
    lo = MOE_GROUPS + MOE_PER_GROUP * gidx
    l2 = jnp.where((lane >= lo) & (lane < lo + MOE_PER_GROUP), logits, NEG_INF)
    v1 = jnp.max(l2, axis=-1, keepdims=True)
    i1 = jnp.min(jnp.where(l2 == v1, lane, big), axis=-1, keepdims=True)
    l2 = jnp.where(lane == i1, NEG_INF, l2)
    v2 = jnp.max(l2, axis=-1, keepdims=True)
    i2 = jnp.min(jnp.where(l2 == v2, lane, big), axis=-1, keepdims=True)
    e = jnp.exp(v2 - v1)
    wa = gval / (1.0 + e)
    wb = wa * e
    first = i1 <= i2
    w_lo, w_hi = jnp.where(first, wa, wb), jnp.where(first, wb, wa)
    e_lo = (jnp.minimum(i1, i2) - MOE_GROUPS).astype(F32)
    e_hi = (jnp.maximum(i1, i2) - MOE_GROUPS).astype(F32)
    rec = jnp.where(lane == 0, w_lo, jnp.where(lane == 1, w_hi, jnp.where(lane == 2, e_lo,
                    jnp.where(lane == 3, e_hi, 0.0))))
    d = h.shape[-1]
    hx_ref[0, rows, 0:d] = h
    hx_ref[0, rows, d:d + ROUTE_LANES] = rec
    rec_ref[0, rows, :] = rec


def _router_tables(r1_w, r1_b, r2_w, r2_b):
    d = r1_w.shape[0]
    wr = jnp.zeros((d, ROUTE_LANES), F32)
    wr = wr.at[:, :MOE_GROUPS].set(r1_w)
    wr = wr.at[:, MOE_GROUPS:MOE_GROUPS + MOE_EXPERTS].set(jnp.transpose(r2_w, (1, 0, 2)).reshape(d, MOE_EXPERTS))
    br = jnp.zeros((1, ROUTE_LANES), F32)
    br = br.at[0, :MOE_GROUPS].set(r1_b)
    br = br.at[0, MOE_GROUPS:MOE_GROUPS + MOE_EXPERTS].set(r2_b.reshape(MOE_EXPERTS))
    wr_hi = wr.astype(BF16)
    wr_lo = (wr - wr_hi.astype(F32)).astype(BF16)
    return jnp.concatenate([wr_hi, wr_lo], axis=1), br


def _merge_kernel(x_ref, ga_ref, hf_ref, hb_ref, ub_ref, ys_ref, d_ref, gw_ref, gb_ref, wo_ref, gate_ref,
                  g2_ref, sh_ref, sc_ref, wr_ref, br_ref, xo_ref, h_ref, rec_ref):
    w = ga_ref.shape[-1]
    subs = _sub_rows(x_ref.shape[1])
    y_a = [((hf_ref[0, r, :] + hb_ref[0, r, :]) * _gelu(ga_ref[0, r, :])).astype(BF16) for r in subs]
    y_s = [_gelu(ys_ref[0, r, :] + d_ref[...] * ub_ref[0, r, :]) for r in subs]
    glu = [_bdot(v, gw_ref[...]) for v in y_s]
    y_s = [(v * jax.nn.sigmoid(g + gb_ref[...])).astype(BF16) for v, g in zip(y_s, glu)]
    y = [jnp.dot(a, wo_ref[0:w, :], preferred_element_type=F32)
         + jnp.dot(s, wo_ref[w:2 * w, :], preferred_element_type=F32) for a, s in zip(y_a, y_s)]
    h = []
    for r, v in zip(subs, y):
        x = x_ref[0, r, :] + gate_ref[0] * v
        xo_ref[0, r, :] = x
        h.append(_norm_mod(x, g2_ref[...], sh_ref[0], sc_ref[0]))
    logits = [_route_logits(v, wr_ref, br_ref) for v in h]
    for r, v, lg in zip(subs, h, logits):
        _route_select(v, lg, h_ref, rec_ref, r)


def _merge(x, ga, hf, hb, ub, ys, s5_d, glu_w, glu_b, w_out, gate, g2, shift, scale, wr, br):
    b, t, d = x.shape
    w = ga.shape[-1]
    tm = min(TOKEN_TILE, t)
    tok = lambda n: pl.BlockSpec((1, tm, n), lambda bi, i: (bi, i, 0))
    row = pl.BlockSpec((1, 1, d), lambda bi, i: (bi, 0, 0))
    full = lambda shape: pl.BlockSpec(shape, lambda bi, i: (0,) * len(shape))
    return pl.pallas_call(
        _merge_kernel,
        grid=(b, t // tm),
        in_specs=[tok(d), tok(w), tok(w), tok(w), tok(w), tok(w), full((1, w)), full((w, w)), full((1, w)),
                  full((2 * w, d)), row, full((1, d)), row, row, full(wr.shape), full(br.shape)],
        out_specs=[tok(d), tok(d + ROUTE_LANES), tok(ROUTE_LANES)],
        out_shape=[jax.ShapeDtypeStruct((b, t, d), F32), jax.ShapeDtypeStruct((b, t, d + ROUTE_LANES), F32),
                   jax.ShapeDtypeStruct((b, t, ROUTE_LANES), F32)],
        compiler_params=_cparams("parallel", "parallel"),
        name="merge_route",
    )(x, ga, hf, hb, ub, ys, s5_d, glu_w, glu_b, w_out, gate, g2, shift, scale, wr, br)


DMA_GROUP = 8


EXPERT_CHUNK = 256


GATHER_AHEAD = 2


def _expert_kernel(elo_ref, ehi_ref, nrows_ref, g0_ref, g1_ref, g2_ref, sprv_ref, scur_ref, hx_hbm,
                   gl_ref, ul_ref, dl_ref, gh_ref, uh_ref, dh_ref, y_hbm, xb, yb, gsem, ssem):
    t = pl.program_id(0)
    nt = pl.num_programs(0)
    n = nrows_ref[t]
    _, tm, d = yb.shape
    f = gl_ref.shape[-1]
    nxb = GATHER_AHEAD + 1

    def gather_row(iref, r, s):
        return pltpu.make_async_copy(hx_hbm.at[pl.ds(iref[0, 0, r], 1)], xb.at[s, pl.ds(r, 1)], gsem.at[s])

    def scatter_row(iref, r, s):
        return pltpu.make_async_copy(yb.at[s, pl.ds(r, 1)], y_hbm.at[pl.ds(iref[0, 0, r], 1)], ssem.at[s])

    def gather_wait(s):
        pltpu.make_async_copy(hx_hbm.at[pl.ds(0, tm)], xb.at[s], gsem.at[s]).wait()

    def scatter_wait(s):
        pltpu.make_async_copy(yb.at[s], y_hbm.at[pl.ds(0, tm)], ssem.at[s]).wait()

    def all_rows(fn):
        g = DMA_GROUP
        lax.fori_loop(0, tm // g, lambda k, c: ([fn(k * g + j) for j in range(g)], c)[1], 0)

    @pl.when(t == 0)
    def _():
        yb[...] = jnp.zeros_like(yb)
        first_trash = pltpu.make_async_copy(yb.at[0], y_hbm.at[pl.ds(y_hbm.shape[0] - 2 * tm, tm)], ssem.at[0])
        first_trash.start()
        first_trash.wait()
        all_rows(lambda r: gather_row(g0_ref, r, 0).start())
        all_rows(lambda r: gather_row(g1_ref, r, 1).start())

    s = t % 2
    o = 1 - s
    cur = t % nxb
    nx1 = (t + 1) % nxb
    nx2 = (t + 2) % nxb

    def step():
        gather_wait(cur)
        x = xb[cur, :, 0:d].astype(BF16)
        gates = (xb[cur, :, d:d + 1], xb[cur, :, d + 1:d + 2])
        nchunk = f // EXPERT_CHUNK
        per = tm // (2 * nchunk)
        hids = []
        for e, (g_ref, u_ref) in enumerate(((gl_ref, ul_ref), (gh_ref, uh_ref))):
            for k in range(nchunk):
                c = e * nchunk + k
                for r in range(c * per, (c + 1) * per):
                    gather_row(g2_ref, r, nx2).start()
                    scatter_row(sprv_ref, r, o).start()
                cs = slice(k * EXPERT_CHUNK, (k + 1) * EXPERT_CHUNK)
                hid = _silu(jnp.dot(x, g_ref[0, :, cs], preferred_element_type=F32)) * jnp.dot(
                    x, u_ref[0, :, cs], preferred_element_type=F32)
                hids.append((hid * gates[e]).astype(BF16))
        y = None
        for e, d_ref in enumerate((dl_ref, dh_ref)):
            for k in range(nchunk):
                part = jnp.dot(hids[e * nchunk + k], d_ref[0, k * EXPERT_CHUNK:(k + 1) * EXPERT_CHUNK, :],
                               preferred_element_type=F32)
                y = part if y is None else y + part

        @pl.when(t > 0)
        def _():
            scatter_wait(s)
        yb[s] = y

        @pl.when(t == nt - 1)
        def _():
            gather_wait(nx1)
            gather_wait(nx2)
            scatter_wait(o)
            all_rows(lambda r: scatter_row(scur_ref, r, s).start())
            scatter_wait(s)

    def drain():
        gather_wait(cur)
        gather_wait(nx1)
        scatter_wait(s)
        all_rows(lambda r: scatter_row(sprv_ref, r, o).start())
        scatter_wait(o)

    had_rows = nrows_ref[jnp.maximum(t - 1, 0)] > 0
    pl.when(n > 0)(step)
    pl.when((n == 0) & (t > 0) & had_rows)(drain)


def _moe_schedule(rec, tm):
    n = rec.shape[0]
    lo = jnp.clip(rec[:, 2].astype(jnp.int32), 0, MOE_EXPERTS - 1)
    hi = jnp.clip(rec[:, 3].astype(jnp.int32), 0, MOE_EXPERTS - 1)
    nbk = MOE_EXPERTS * MOE_EXPERTS
    bucket = lo * MOE_EXPERTS + hi
    order = jnp.argsort(bucket, stable=True).astype(jnp.int32)
    eids = jnp.arange(MOE_EXPERTS, dtype=jnp.int32)[None, :]
    count = jnp.einsum('nl,nh->lh', (lo[:, None] == eids).astype(F32), (hi[:, None] == eids).astype(F32),
                       precision=HIGHEST).astype(jnp.int32).reshape(nbk)
    start = jnp.cumsum(count) - count
    tiles = (count + tm - 1) // tm
    tile_end = jnp.cumsum(tiles)
    nt = n // tm + MOE_PAIRS
    tix = jnp.arange(nt, dtype=jnp.int32)
    total = tile_end[-1]
    bk = jnp.sum((tile_end[None, :] <= jnp.minimum(tix, total - 1)[:, None]).astype(jnp.int32), axis=1)
    bk = jnp.clip(bk, 0, nbk - 1)
    k = tix - (tile_end[bk] - tiles[bk])
    nrows = jnp.where(tix < total, jnp.clip(count[bk] - k * tm, 0, tm), 0).astype(jnp.int32)
    r = jnp.arange(tm, dtype=jnp.int32)[None, :]
    pos = jnp.clip(start[bk][:, None] + k[:, None] * tm + r, 0, n - 1)
    gidx = order[pos]
    sidx = jnp.where(r < nrows[:, None], gidx, n + (tix % 2)[:, None] * tm + r)
    sprev = jnp.concatenate([n + tm + r, sidx[:-1]], axis=0)
    shape = (nt, 1, tm)
    return (bk // MOE_EXPERTS, bk % MOE_EXPERTS, nrows, gidx.reshape(shape), sprev.reshape(shape),
            sidx.reshape(shape))


def _experts(hx, rec, w_gate, w_up, w_down):
    n, dx = hx.shape
    d = dx - ROUTE_LANES
    f = w_gate.shape[-1]
    tm = EXPERT_TILE
    elo, ehi, nrows, gidx, sprev, sidx = _moe_schedule(rec, tm)
    nt = nrows.shape[0]
    wspec = lambda shape, which: pl.BlockSpec(
        (1,) + shape, (lambda t, elo, ehi, nr: (elo[t], 0, 0)) if which == 0 else (lambda t, elo, ehi, nr: (ehi[t], 0, 0)))
    ispec = lambda m: pl.BlockSpec((1, 1, tm), m, memory_space=pltpu.SMEM)
    gs = pltpu.PrefetchScalarGridSpec(
        num_scalar_prefetch=3,
        grid=(nt,),
        in_specs=[ispec(lambda t, *_: (t, 0, 0)), ispec(lambda t, *_: (jnp.minimum(t + 1, nt - 1), 0, 0)),
                  ispec(lambda t, *_: (jnp.minimum(t + 2, nt - 1), 0, 0)),
                  ispec(lambda t, *_: (t, 0, 0)), ispec(lambda t, *_: (t, 0, 0)),
                  pl.BlockSpec(memory_space=pl.ANY),
                  wspec((d, f), 0), wspec((d, f), 0), wspec((f, d), 0),
                  wspec((d, f), 1), wspec((d, f), 1), wspec((f, d), 1)],
        out_specs=pl.BlockSpec(memory_space=pl.ANY),
        scratch_shapes=[pltpu.VMEM((GATHER_AHEAD + 1, tm, dx), F32), pltpu.VMEM((2, tm, d), F32),
                        pltpu.SemaphoreType.DMA((GATHER_AHEAD + 1,)), pltpu.SemaphoreType.DMA((2,))])
    return pl.pallas_call(
        _expert_kernel,
        grid_spec=gs,
        out_shape=jax.ShapeDtypeStruct((n + 2 * tm, d), F32),
        compiler_params=_cparams("arbitrary"),
        name="experts",
    )(elo, ehi, nrows, gidx, gidx, gidx, sprev, sidx, hx, w_gate, w_up, w_down, w_gate, w_up, w_down)


def _qkv_kernel(x_ref, y_ref, gate_ref, g_ref, sh_ref, sc_ref, w_ref, xo_ref, q_ref, k_ref, v_ref, *, qscale):
    x = x_ref[0] + gate_ref[0] * y_ref[...]
    xo_ref[0] = x
    h = _norm_mod(x, g_ref[...], sh_ref[0], sc_ref[0])
    r = _bdot(h, w_ref[...])
    d = x.shape[-1]
    npair = q_ref.shape[1]
    lanes = q_ref.shape[-1]
    for p in range(npair):
        q_ref[0, p] = (r[:, p * lanes:(p + 1) * lanes] * qscale).astype(BF16)
        k_ref[0, p] = r[:, d + p * lanes:d + (p + 1) * lanes].astype(BF16)
        v_ref[0, p] = r[:, 2 * d + p * lanes:2 * d + (p + 1) * lanes].astype(BF16)


def _flat_rows(row0, t, tm, d):
    return pl.BlockSpec((tm, d), lambda bi, i: (row0 // tm + bi * (t // tm) + i, 0))


def _qkv(x, y, row0, gate, g, shift, scale, w_qkv):
    b, t, d = x.shape
    tm = min(TOKEN_TILE, t)
    assert row0 % tm == 0
    npair = NA_HEADS // 2
    lanes = d // npair
    tok = pl.BlockSpec((1, tm, d), lambda bi, i: (bi, i, 0))
    row = pl.BlockSpec((1, 1, d), lambda bi, i: (bi, 0, 0))
    hp = pl.BlockSpec((1, npair, tm, lanes), lambda bi, i: (bi, 0, i, 0))
    hps = jax.ShapeDtypeStruct((b, npair, t, lanes), BF16)
    return pl.pallas_call(
        functools.partial(_qkv_kernel, qscale=float((d // NA_HEADS) ** -0.5)),
        grid=(b, t // tm),
        in_specs=[tok, _flat_rows(row0, t, tm, d), row, pl.BlockSpec((1, d), lambda bi, i: (0, 0)), row, row,
                  pl.BlockSpec(w_qkv.shape, lambda bi, i: (0, 0))],
        out_specs=[tok, hp, hp, hp],
        out_shape=[jax.ShapeDtypeStruct((b, t, d), F32), hps, hps, hps],
        compiler_params=_cparams("parallel", "parallel"),
        name="qkv",
    )(x, y, gate, g, shift, scale, w_qkv)


def _na_bias_tables(rpb):
    h = rpb.shape[0]
    col = jnp.arange(GRID_W)
    c_start = jnp.clip(col - NA_KC // 2, 0, GRID_W - NA_KC)
    col_ok = (col[None, :] >= c_start[:, None]) & (col[None, :] < c_start[:, None] + NA_KC)
    dc_idx = jnp.clip(col[None, :] - col[:, None] + NA_KC - 1, 0, 2 * NA_KC - 2)
    rc = jnp.where(col_ok[None, None], rpb[:, :, dc_idx].astype(F32), NEG_INF)
    dv = jnp.arange(NA_KR)[:, None]
    kr = jnp.arange(NA_KR)[None, :]
    t = rc[:, kr - dv + NA_KR - 1]
    t = t.reshape(h // 2, 2, NA_KR, NA_KR, GRID_W, GRID_W)
    return jnp.transpose(t, (0, 2, 1, 4, 3, 5)).reshape(h // 2, NA_KR, 2 * GRID_W, NA_KR * GRID_W)


def _attn_kernel(q_ref, kp_ref, kc_ref, kn_ref, vp_ref, vc_ref, vn_ref, kx_ref, vx_ref, bias_ref, o_ref,
                 kbuf, vbuf, q2_ref, sc_ref, pl_ref, pc_ref, li_ref, *, rows):
    i = pl.program_id(2)
    blk = kc_ref.shape[2]
    kbuf[0:blk, :] = kp_ref[0, 0]
    kbuf[blk:2 * blk, :] = kc_ref[0, 0]
    kbuf[2 * blk:3 * blk, :] = kn_ref[0, 0]
    vbuf[0:blk, :] = vp_ref[0, 0]
    vbuf[blk:2 * blk, :] = vc_ref[0, 0]
    vbuf[2 * blk:3 * blk, :] = vn_ref[0, 0]
    lanes = q_ref.shape[-1]
    w2 = 2 * GRID_W
    lane = lax.broadcasted_iota(jnp.int32, (GRID_W, lanes), 1)
    first = lane < lanes // 2
    nt_dims = (((1,), (1,)), ((), ()))
    for rho in range(NA_KR):
        q = q_ref[0, 0, rho * GRID_W:(rho + 1) * GRID_W, :]
        q2_ref[rho * w2:rho * w2 + GRID_W, :] = jnp.where(first, q, jnp.zeros_like(q))
        q2_ref[rho * w2 + GRID_W:(rho + 1) * w2, :] = jnp.where(first, jnp.zeros_like(q), q)
    sc_ref[...] = lax.dot_general(q2_ref[...], kx_ref[0, 0], nt_dims, preferred_element_type=F32)

    def window(rho):
        r = NA_KR * i + rho
        rs = jnp.clip(r - NA_KR // 2, 0, rows - NA_KR)
        return pl.multiple_of((rs - NA_KR * i + NA_KR) * GRID_W, GRID_W), r - rs

    for rho in range(NA_KR):
        wstart, dvar = window(rho)
        sl = slice(rho * w2, (rho + 1) * w2)
        s_loc = lax.dot_general(q2_ref[sl, :], kbuf[pl.ds(wstart, blk), :], nt_dims,
                                preferred_element_type=F32) + bias_ref[0, dvar]
        s_ctx = sc_ref[sl, :]
        m = jnp.maximum(jnp.max(s_loc, axis=-1, keepdims=True), jnp.max(s_ctx, axis=-1, keepdims=True))
        p_loc = jnp.exp(s_loc - m)
        p_ctx = jnp.exp(s_ctx - m)
        den = jnp.sum(p_loc, axis=-1, keepdims=True) + jnp.sum(p_ctx, axis=-1, keepdims=True)
        pl_ref[rho] = p_loc.astype(BF16)
        pc_ref[sl, :] = p_ctx.astype(BF16)
        li_ref[sl, :] = jnp.broadcast_to(1.0 / den, (w2, lanes))

    o_ctx = jnp.dot(pc_ref[...], vx_ref[0, 0], preferred_element_type=F32)
    for rho in range(NA_KR):
        wstart, _ = window(rho)
        sl = slice(rho * w2, (rho + 1) * w2)
        o = jnp.dot(pl_ref[rho], vbuf[pl.ds(wstart, blk), :], preferred_element_type=F32)
        o = (o + o_ctx[sl, :]) * li_ref[sl, :]
        o_ref[0, rho * GRID_W:(rho + 1) * GRID_W, :] = jnp.where(first, o[:GRID_W], o[GRID_W:]).astype(o_ref.dtype)


def _attention(q, k, v, kx, vx, bias):
    b, npair, t, lanes = q.shape
    c = kx.shape[2]
    blk = NA_KR * GRID_W
    nb = t // blk
    rows = t // GRID_W
    cur = lambda p, bi, i: (bi, p, i, 0)
    prv = lambda p, bi, i: (bi, p, jnp.maximum(i - 1, 0), 0)
    nxt = lambda p, bi, i: (bi, p, jnp.minimum(i + 1, nb - 1), 0)
    tb = lambda m: pl.BlockSpec((1, 1, blk, lanes), m)
    cx = pl.BlockSpec((1, 1, c, lanes), lambda p, bi, i: (bi, p, 0, 0))
    return pl.pallas_call(
        functools.partial(_attn_kernel, rows=rows),
        grid=(npair, b, nb),
        in_specs=[tb(cur), tb(prv), tb(cur), tb(nxt), tb(prv), tb(cur), tb(nxt), cx, cx,
                  pl.BlockSpec((1,) + bias.shape[1:], lambda p, bi, i: (p, 0, 0, 0))],
        out_specs=pl.BlockSpec((1, blk, lanes), lambda p, bi, i: (bi, i, p)),
        out_shape=jax.ShapeDtypeStruct((b, t, npair * lanes), BF16),
        scratch_shapes=[pltpu.VMEM((3 * blk, lanes), BF16), pltpu.VMEM((3 * blk, lanes), BF16),
                        pltpu.VMEM((2 * blk, lanes), BF16), pltpu.VMEM((2 * blk, c), F32),
                        pltpu.VMEM((NA_KR, 2 * GRID_W, blk), BF16), pltpu.VMEM((2 * blk, c), BF16),
                        pltpu.VMEM((2 * blk, lanes), F32)],
        compiler_params=_cparams("parallel", "parallel", "parallel"),
        name="na_attention",
    )(q, k, k, k, v, v, v, kx, vx, bias)


def _oproj_kernel(x_ref, o_ref, wo_ref, gate_ref, g2_ref, sh_ref, sc_ref, wr_ref, br_ref, xo_ref, h_ref, rec_ref):
    subs = _sub_rows(x_ref.shape[1])
    y = [jnp.dot(o_ref[0, r, :], wo_ref[...], preferred_element_type=F32) for r in subs]
    h = []
    for r, v in zip(subs, y):
        x = x_ref[0, r, :] + gate_ref[0] * v
        xo_ref[0, r, :] = x
        h.append(_norm_mod(x, g2_ref[...], sh_ref[0], sc_ref[0]))
    logits = [_route_logits(v, wr_ref, br_ref) for v in h]
    for r, v, lg in zip(subs, h, logits):
        _route_select(v, lg, h_ref, rec_ref, r)


def _oproj(x, o, w_out, gate, g2, shift, scale, wr, br):
    b, t, d = x.shape
    tm = min(TOKEN_TILE, t)
    tok = lambda n: pl.BlockSpec((1, tm, n), lambda bi, i: (bi, i, 0))
    row = pl.BlockSpec((1, 1, d), lambda bi, i: (bi, 0, 0))
    full = lambda shape: pl.BlockSpec(shape, lambda bi, i: (0,) * len(shape))
    return pl.pallas_call(
        _oproj_kernel,
        grid=(b, t // tm),
        in_specs=[tok(d), tok(d), full((d, d)), row, full((1, d)), row, row, full(wr.shape), full(br.shape)],
        out_specs=[tok(d), tok(d + ROUTE_LANES), tok(ROUTE_LANES)],
        out_shape=[jax.ShapeDtypeStruct((b, t, d), F32), jax.ShapeDtypeStruct((b, t, d + ROUTE_LANES), F32),
                   jax.ShapeDtypeStruct((b, t, ROUTE_LANES), F32)],
        compiler_params=_cparams("parallel", "parallel"),
        name="oproj_route",
    )(x, o, w_out, gate, g2, shift, scale, wr, br)


def _final_kernel(x_ref, y_ref, gate_ref, g_ref, o_ref):
    x = x_ref[0] + gate_ref[0] * y_ref[...]
    ms = jnp.mean(x * x, axis=-1, keepdims=True)
    o_ref[0] = (x * lax.rsqrt(ms + RMS_EPS)) * g_ref[...]


def _final(x, y, gate, g):
    b, t, d = x.shape
    tm = min(TOKEN_TILE, t)
    tok = pl.BlockSpec((1, tm, d), lambda bi, i: (bi, i, 0))
    return pl.pallas_call(
        _final_kernel,
        grid=(b, t // tm),
        in_specs=[tok, _flat_rows(0, t, tm, d), pl.BlockSpec((1, 1, d), lambda bi, i: (bi, 0, 0)),
                  pl.BlockSpec((1, d), lambda bi, i: (0, 0))],
        out_specs=tok,
        out_shape=jax.ShapeDtypeStruct((b, t, d), F32),
        compiler_params=_cparams("parallel", "parallel"),
        name="final_norm",
    )(x, y, gate, g)


def kernel(x, c, ctx, c_ctx, ada_w, ada_b, norm1_g, norm2_g, rec_w_in, rec_conv_w, rec_conv_b, lru_wa, lru_ba, lru_wx, lru_bx, lru_lambda, s5_a_re, s5_a_im, s5_log_dt, s5_b_re, s5_b_im, s5_c_re, s5_c_im, s5_d, s5_glu_w, s5_glu_b, rec_w_out, na_w_qkv, na_w_out, na_rpb, moe_r1_w, moe_r1_b, moe_r2_w, moe_r2_b, moe_w_gate, moe_w_up, moe_w_down, final_norm_g):
    b, t, d = x.shape
    tc = ctx.shape[1]
    assert ada_w.shape[0] == 2, "layer 0 recurrent mixer, layer 1 neighbourhood attention"
    w = rec_w_in.shape[-1] // 3

    rpad = -(b + 1) % 8
    cc = jnp.concatenate([c, c_ctx[None], jnp.zeros((rpad, d), F32)], axis=0)
    mod = _ada_mod(cc, ada_w, ada_b)

    def mods(layer, ctx_rows):
        rows = jnp.broadcast_to(mod[layer, b:b + 1], (b, 6 * d)) if ctx_rows else mod[layer, :b]
        return [rows[:, j * d:(j + 1) * d].reshape(b, 1, d) for j in range(6)]

    row = lambda v: v.reshape(1, -1)

    w_in = rec_w_in[0].astype(BF16)
    lru = [(_block_diag(lru_wa[0, dr]).astype(BF16), row(lru_ba[0, dr]), _block_diag(lru_wx[0, dr]).astype(BF16),
            row(lru_bx[0, dr]), row(lru_lambda[0, dr])) for dr in (0, 1)]
    s5t = _s5_tables(s5_a_re[0], s5_a_im[0], s5_log_dt[0], s5_b_re[0], s5_b_im[0], s5_c_re[0], s5_c_im[0])
    glu_w = s5_glu_w[0].astype(BF16)
    w_out0 = rec_w_out[0].astype(BF16)
    wr0, br0 = _router_tables(moe_r1_w[0], moe_r1_b[0], moe_r2_w[0], moe_r2_b[0])

    def mixer0(xs, m, h0_lru, h0_s5):
        xa, ga, ub = _inproj(xs, row(norm1_g[0]), m[0], m[1], w_in)
        hf = _lru_dir(xa, rec_conv_w[0], row(rec_conv_b[0]), *lru[0], h0_lru[0], False)
        hb = _lru_dir(xa, rec_conv_w[0], row(rec_conv_b[0]), *lru[1], h0_lru[1], True)
        ys, s5_fin = _s5(ub, s5t, h0_s5)
        x_mid, h2, rec = _merge(xs, ga, hf, hb, ub, ys, row(s5_d[0]), glu_w, row(s5_glu_b[0]), w_out0, m[2],
                                row(norm2_g[0]), m[3], m[4], wr0, br0)
        return x_mid, h2, rec, (hf[:, -1:], hb[:, :1]), s5_fin

    zl = jnp.zeros((b, 1, w), F32)
    zs = jnp.zeros((s5t[0].shape[0], b, s5t[0].shape[-1]), F32)
    mc0, ml0 = mods(0, True), mods(0, False)
    xc_mid, hc2, recc, lru_fin, s5_fin = mixer0(ctx, mc0, (zl, zl), zs)
    xl_mid, hl2, recl, _, _ = mixer0(x, ml0, lru_fin, s5_fin)

    h_all = jnp.concatenate([hl2.reshape(b * t, -1), hc2.reshape(b * tc, -1)], axis=0)
    rec_all = jnp.concatenate([recl.reshape(b * t, -1), recc.reshape(b * tc, -1)], axis=0)
    y_all = _experts(h_all, rec_all, moe_w_gate[0].astype(BF16), moe_w_up[0].astype(BF16),
                     moe_w_down[0].astype(BF16))

    w_qkv = na_w_qkv[0].astype(BF16)
    mc1, ml1 = mods(1, True), mods(1, False)
    _, _, kx, vx = _qkv(xc_mid, y_all, b * t, mc0[5], row(norm1_g[1]), mc1[0], mc1[1], w_qkv)
    x1, q, k, v = _qkv(xl_mid, y_all, 0, ml0[5], row(norm1_g[1]), ml1[0], ml1[1], w_qkv)
    o = _attention(q, k, v, kx, vx, _na_bias_tables(na_rpb[0]))
    wr1, br1 = _router_tables(moe_r1_w[1], moe_r1_b[1], moe_r2_w[1], moe_r2_b[1])
    x1_mid, h2, rec = _oproj(x1, o, na_w_out[0].astype(BF16), ml1[2], row(norm2_g[1]), ml1[3], ml1[4], wr1, br1)
    y1 = _experts(h2.reshape(b * t, -1), rec.reshape(b * t, -1), moe_w_gate[1].astype(BF16),
                  moe_w_up[1].astype(BF16), moe_w_down[1].astype(BF16))
    return _final(x1_mid, y1, ml1[5], row(final_norm_g))
```

```python
import functools

import jax
import jax.numpy as jnp
from jax import lax
from jax.experimental import pallas as pl
from jax.experimental.pallas import tpu as pltpu

F32 = jnp.float32
BF16 = jnp.bfloat16
HIGHEST = lax.Precision.HIGHEST

RMS_EPS = 1e-6
GRID_W = 64
LRU_HEADS = 8
LRU_C = 8.0
S5_GROUP = 16
S5_CHUNK = 16
NA_HEADS = 16
NA_KR = 8
NA_KC = 16
NEG_INF = -1e30
MOE_GROUPS = 4
MOE_PER_GROUP = 8
MOE_EXPERTS = MOE_GROUPS * MOE_PER_GROUP
MOE_PAIRS = MOE_GROUPS * (MOE_PER_GROUP * (MOE_PER_GROUP - 1) // 2)
EXPERT_TILE = 256
ROUTE_LANES = 128
REC_FIELDS = 8
TOKEN_TILE = 512
VMEM_LIMIT = 56 * 1024 * 1024


def _cparams(*sem):
    return pltpu.CompilerParams(dimension_semantics=sem, vmem_limit_bytes=VMEM_LIMIT)


def _norm_mod(x, g, shift, scale):
    ms = jnp.mean(x * x, axis=-1, keepdims=True)
    return (x * lax.rsqrt(ms + RMS_EPS)) * g * (1.0 + scale) + shift


def _silu(x):
    return x * jax.nn.sigmoid(x)


def _gelu(x):
    return jax.nn.gelu(x, approximate=True)


def _bdot(a, b):
    return jnp.dot(a.astype(BF16), b, preferred_element_type=F32)


def _ada_kernel(c_ref, w_ref, b_ref, o_ref):
    o_ref[0] = jnp.dot(_silu(c_ref[...]), w_ref[0], preferred_element_type=F32,
                       precision=HIGHEST) + b_ref[0]


def _ada_mod(cc, ada_w, ada_b):
    n_layers, d, d6 = ada_w.shape
    r = cc.shape[0]
    return pl.pallas_call(
        _ada_kernel,
        grid=(n_layers, d6 // d),
        in_specs=[pl.BlockSpec((r, d), lambda l, j: (0, 0)),
                  pl.BlockSpec((1, d, d), lambda l, j: (l, 0, j)),
                  pl.BlockSpec((1, 1, d), lambda l, j: (l, 0, j))],
        out_specs=pl.BlockSpec((1, r, d), lambda l, j: (l, 0, j)),
        out_shape=jax.ShapeDtypeStruct((n_layers, r, d6), F32),
        compiler_params=_cparams("arbitrary", "arbitrary"),
        name="ada_mod",
    )(cc, ada_w, ada_b.reshape(n_layers, 1, d6))


def _inproj_kernel(x_ref, g_ref, sh_ref, sc_ref, w_ref, xa_ref, ga_ref, ub_ref):
    h = _norm_mod(x_ref[0], g_ref[...], sh_ref[0], sc_ref[0])
    r = _bdot(h, w_ref[...])
    w = xa_ref.shape[-1]
    xa_ref[0] = r[:, :w]
    ga_ref[0] = r[:, w:2 * w]
    ub_ref[0] = r[:, 2 * w:]


def _inproj(x, g, shift, scale, w_in):
    b, t, d = x.shape
    w = w_in.shape[1] // 3
    tm = min(TOKEN_TILE, t)
    row = pl.BlockSpec((1, 1, d), lambda bi, i: (bi, 0, 0))
    out = pl.BlockSpec((1, tm, w), lambda bi, i: (bi, i, 0))
    return pl.pallas_call(
        _inproj_kernel,
        grid=(b, t // tm),
        in_specs=[pl.BlockSpec((1, tm, d), lambda bi, i: (bi, i, 0)),
                  pl.BlockSpec((1, d), lambda bi, i: (0, 0)), row, row,
                  pl.BlockSpec(w_in.shape, lambda bi, i: (0, 0))],
        out_specs=[out, out, out],
        out_shape=[jax.ShapeDtypeStruct((b, t, w), F32)] * 3,
        compiler_params=_cparams("parallel", "parallel"),
        name="inproj",
    )(x, g, shift, scale, w_in)


def _lru_kernel(xc_ref, xp_ref, xn_ref, cw_ref, cb_ref, wa_ref, ba_ref, wx_ref, bx_ref, lam_ref,
                h0_ref, o_ref, ext_ref, a_ref, b_ref, car_ref, *, reverse, nt, tt):
    i = pl.program_id(1)
    ti = (nt - 1 - i) if reverse else i
    w = o_ref.shape[-1]

    @pl.when(i == 0)
    def _():
        car_ref[...] = h0_ref[0]

    ext_ref[0:8, :] = jnp.where(ti == 0, 0.0, xp_ref[0])
    ext_ref[8:8 + tt, :] = xc_ref[0]
    ext_ref[8 + tt:16 + tt, :] = jnp.where(ti == nt - 1, 0.0, xn_ref[0])
    cw = cw_ref[...]
    u = (ext_ref[6:6 + tt, :] * cw[0:1] + ext_ref[7:7 + tt, :] * cw[1:2]
         + ext_ref[8:8 + tt, :] * cw[2:3] + ext_ref[9:9 + tt, :] * cw[3:4]) + cb_ref[...]
    r = jax.nn.sigmoid(_bdot(u, wa_ref[...]) + ba_ref[...])
    ig = jax.nn.sigmoid(_bdot(u, wx_ref[...]) + bx_ref[...])
    log_a = (-LRU_C) * r * jax.nn.softplus(-lam_ref[...])
    a = jnp.exp(log_a)
    a_ref[...] = a
    b_ref[...] = jnp.sqrt(-jnp.tanh(log_a) * (1.0 + a * a)) * (ig * u)

    nsl = tt // 8
    row = lax.broadcasted_iota(jnp.int32, (8, w), 0)

    def slab(s, carry):
        off = pl.multiple_of(((nsl - 1 - s) if reverse else s) * 8, 8)
        a = a_ref[pl.ds(off, 8), :]
        bb = b_ref[pl.ds(off, 8), :]
        for k in (1, 2, 4):
            valid = (row < 8 - k) if reverse else (row >= k)
            sh = (8 - k) if reverse else k
            a_s = jnp.where(valid, pltpu.roll(a, sh, 0), 1.0)
            b_s = jnp.where(valid, pltpu.roll(bb, sh, 0), 0.0)
            bb = bb + a * b_s
            a = a * a_s
        h = bb + a * carry
        o_ref[0, pl.ds(off, 8), :] = h
        return h[0:1] if reverse else h[7:8]

    car_ref[...] = lax.fori_loop(0, nsl, slab, car_ref[...])


def _lru_dir(xa, conv_w, conv_b, wa_bd, ba, wx_bd, bx, lam, h0, reverse):
    b, t, w = xa.shape
    tt = min(TOKEN_TILE, t)
    nt = t // tt
    hb = tt // 8
    tile = (lambda i: nt - 1 - i) if reverse else (lambda i: i)
    full = lambda shape: pl.BlockSpec(shape, lambda bi, i: (0,) * len(shape))
    return pl.pallas_call(
        functools.partial(_lru_kernel, reverse=reverse, nt=nt, tt=tt),
        grid=(b, nt),
        in_specs=[pl.BlockSpec((1, tt, w), lambda bi, i: (bi, tile(i), 0)),
                  pl.BlockSpec((1, 8, w), lambda bi, i: (bi, jnp.maximum(tile(i) * hb - 1, 0), 0)),
                  pl.BlockSpec((1, 8, w), lambda bi, i: (bi, jnp.minimum((tile(i) + 1) * hb, t // 8 - 1), 0)),
                  full(conv_w.shape), full((1, w)), full((w, w)), full((1, w)), full((w, w)),
                  full((1, w)), full((1, w)),
                  pl.BlockSpec((1, 1, w), lambda bi, i: (bi, 0, 0))],
        out_specs=pl.BlockSpec((1, tt, w), lambda bi, i: (bi, tile(i), 0)),
        out_shape=jax.ShapeDtypeStruct((b, t, w), F32),
        scratch_shapes=[pltpu.VMEM((tt + 16, w), F32), pltpu.VMEM((tt, w), F32),
                        pltpu.VMEM((tt, w), F32), pltpu.VMEM((1, w), F32)],
        compiler_params=_cparams("parallel", "arbitrary"),
        name="lru_bwd" if reverse else "lru_fwd",
    )(xa, xa, xa, conv_w, conv_b, wa_bd, ba, wx_bd, bx, lam, h0)


def _block_diag(w):
    h, d, _ = w.shape
    eye = jnp.eye(h, dtype=w.dtype)
    return (eye[:, None, :, None] * w[:, :, None, :]).reshape(h * d, h * d)


def _s5_tables(a_re, a_im, log_dt, b_re, b_im, c_re, c_im):
    L = S5_CHUNK
    g, n = a_re.shape[1], a_re.shape[2]
    p = b_re.shape[-1]
    f = lambda x: x.astype(F32)
    a_re, a_im, b_re, b_im, c_re, c_im = map(f, (a_re, a_im, b_re, b_im, c_re, c_im))
    dt = jnp.exp(f(log_dt))[..., None]
    mag = jnp.exp(a_re * dt)
    lb_re, lb_im = mag * jnp.cos(a_im * dt), mag * jnp.sin(a_im * dt)
    den = a_re * a_re + a_im * a_im
    q_re = ((lb_re - 1.0) * a_re + lb_im * a_im) / den
    q_im = (lb_im * a_re - (lb_re - 1.0) * a_im) / den
    bb_re = q_re[..., None] * b_re - q_im[..., None] * b_im
    bb_im = q_re[..., None] * b_im + q_im[..., None] * b_re
    pw_re, pw_im = [jnp.ones_like(lb_re)], [jnp.zeros_like(lb_im)]
    for _ in range(L):
        r_, i_ = pw_re[-1], pw_im[-1]
        pw_re.append(r_ * lb_re - i_ * lb_im)
        pw_im.append(r_ * lb_im + i_ * lb_re)
    pw_re, pw_im = jnp.stack(pw_re, 1), jnp.stack(pw_im, 1)
    es = functools.partial(jnp.einsum, precision=HIGHEST)
    kf = jnp.arange(L - 1, -1, -1)
    kb = jnp.arange(L)
    inc = []
    for d_, ks in ((0, kf), (1, kb)):
        pr, pi = pw_re[d_][ks], pw_im[d_][ks]
        inc.append((pr[..., None] * bb_re[d_] - pi[..., None] * bb_im[d_],
                    pr[..., None] * bb_im[d_] + pi[..., None] * bb_re[d_]))
    gq = 128 // p
    nq = g // gq

    def quad_block_diag(a, outer, rpg, cpg):
        cols = a.shape[-1]
        full = jnp.broadcast_to(a[:, :, None], (nq, outer, gq, rpg, cols)).reshape(nq, outer * gq * rpg, cols)
        rg = (lax.broadcasted_iota(jnp.int32, full.shape, 1) // rpg) % gq
        cg = (lax.broadcasted_iota(jnp.int32, full.shape, 2) // cpg) % gq
        return jnp.where(rg == cg, full, 0.0).astype(BF16)

    inc_all = jnp.stack([jnp.stack(inc[0]), jnp.stack(inc[1])]).reshape(2, 2, L, nq, gq, n, p)
    wb = jnp.transpose(inc_all, (3, 2, 6, 0, 1, 4, 5)).reshape(nq, L, p, 4 * gq * n)
    wb = quad_block_diag(wb, L, p, n)
    rd = []
    for d_, ks in ((0, jnp.arange(1, L + 1)), (1, jnp.arange(L, 0, -1))):
        pr, pi = pw_re[d_][ks], pw_im[d_][ks]
        cl_re = c_re[d_][None] * pr[:, :, None, :] - c_im[d_][None] * pi[:, :, None, :]
        cl_im = c_re[d_][None] * pi[:, :, None, :] + c_im[d_][None] * pr[:, :, None, :]
        rd.append((cl_re, -cl_im))
    rd_all = jnp.stack([jnp.stack(rd[0]), jnp.stack(rd[1])]).reshape(2, 2, L, nq, gq, p, n)
    wc = jnp.transpose(rd_all, (3, 0, 1, 6, 2, 4, 5)).reshape(nq, 4, n, L * gq * p)
    wc = quad_block_diag(wc, 4, n, p)
    ker = []
    for d_ in (0, 1):
        pr, pi = pw_re[d_][:L], pw_im[d_][:L]
        cl_re = c_re[d_][None] * pr[:, :, None, :] - c_im[d_][None] * pi[:, :, None, :]
        cl_im = c_re[d_][None] * pi[:, :, None, :] + c_im[d_][None] * pr[:, :, None, :]
        ker.append(es('kgpn,gnq->kgpq', cl_re, bb_re[d_]) - es('kgpn,gnq->kgpq', cl_im, bb_im[d_]))
    s_i = jnp.arange(L)[:, None]
    t_i = jnp.arange(L)[None, :]
    kf_t = ker[0][jnp.clip(t_i - s_i, 0, L - 1)]
    kb_t = ker[1][jnp.clip(s_i - t_i, 0, L - 1)]
    m = (jnp.where((s_i <= t_i)[:, :, None, None, None], kf_t, 0.0)
         + jnp.where((s_i >= t_i)[:, :, None, None, None], kb_t, 0.0))
    mt = jnp.transpose(m.reshape(L, L, nq, gq, p, p), (2, 0, 5, 1, 3, 4)).reshape(nq, L, p, L * gq * p)
    mt = quad_block_diag(mt, L, p, p)
    l16 = jnp.stack([jnp.stack([pw_re[d_][L], pw_im[d_][L]]) for d_ in (0, 1)])
    l16 = l16.reshape(2, 2, nq, gq * n)
    return wb, mt, wc, l16


def _s5_inc_kernel(x_ref, wb_ref, xs_ref, s_ref):
    lanes = x_ref.shape[-1]
    nj = xs_ref.shape[2]
    for tau in range(S5_CHUNK):
        xs_ref[0, 0, :, tau * lanes:(tau + 1) * lanes] = x_ref[0, pl.ds(tau, nj, stride=S5_CHUNK), :].astype(BF16)
    s_ref[0, 0] = jnp.dot(xs_ref[0, 0], wb_ref[0], preferred_element_type=F32)


def _s5_scan_kernel(sf_ref, sb_ref, l16_ref, h0_ref, hf_ref, hb_ref, hfin_ref, st_ref, *, jb):
    i = pl.program_id(0)
    nq = st_ref.shape[0]
    dl = sf_ref.shape[-1]
    w = dl // 2

    @pl.when(i == 0)
    def _():
        st_ref[...] = h0_ref[...]

    for jj in range(jb):
        for q in range(nq):
            for d, (src, dst, row) in enumerate(((sf_ref, hf_ref, jj), (sb_ref, hb_ref, jb - 1 - jj))):
                s = src[q, :, row, :]
                h = st_ref[q, :, d * dl:(d + 1) * dl]
                dst[q, :, row, :] = h
                lr, li = l16_ref[d, 0, q:q + 1, :], l16_ref[d, 1, q:q + 1, :]
                hr, hi = h[:, :w], h[:, w:]
                st_ref[q, :, d * dl:(d + 1) * dl] = jnp.concatenate(
                    [lr * hr - li * hi + s[:, :w], lr * hi + li * hr + s[:, w:]], axis=-1)

    @pl.when(i == pl.num_programs(0) - 1)
    def _():
        hfin_ref[...] = st_ref[...]


def _s5_out_kernel(xs_ref, hf_ref, hb_ref, mt_ref, wc_ref, y_ref):
    dl = hf_ref.shape[-1]
    y = (jnp.dot(xs_ref[0, 0], mt_ref[0], preferred_element_type=F32)
         + _bdot(hf_ref[0, 0], wc_ref[0, 0:dl, :]) + _bdot(hb_ref[0, 0], wc_ref[0, dl:2 * dl, :]))
    lanes = y_ref.shape[-1]
    for k in range(y_ref.shape[2]):
        y_ref[0, :, k, :] = y[:, k * lanes:(k + 1) * lanes]


def _s5(ub, tables, h0):
    wb, mt, wc, l16 = tables
    b, t, w = ub.shape
    L = S5_CHUNK
    nj = t // L
    nq = wb.shape[0]
    lanes = w // nq
    cl = L * lanes
    sl = wb.shape[-1]
    xs, s = pl.pallas_call(
        _s5_inc_kernel,
        grid=(nq, b),
        in_specs=[pl.BlockSpec((1, t, lanes), lambda q, bi: (bi, 0, q)),
                  pl.BlockSpec((1, cl, sl), lambda q, bi: (q, 0, 0))],
        out_specs=[pl.BlockSpec((1, 1, nj, cl), lambda q, bi: (q, bi, 0, 0)),
                   pl.BlockSpec((1, 1, nj, sl), lambda q, bi: (q, bi, 0, 0))],
        out_shape=[jax.ShapeDtypeStruct((nq, b, nj, cl), BF16), jax.ShapeDtypeStruct((nq, b, nj, sl), F32)],
        compiler_params=_cparams("parallel", "parallel"),
        name="s5_inc",
    )(ub, wb)
    jb = 8
    nblk = nj // jb
    dl = sl // 2
    hf, hb, hfin = pl.pallas_call(
        functools.partial(_s5_scan_kernel, jb=jb),
        grid=(nblk,),
        in_specs=[pl.BlockSpec((nq, b, jb, dl), lambda i: (0, 0, i, 0)),
                  pl.BlockSpec((nq, b, jb, dl), lambda i: (0, 0, nblk - 1 - i, 1)),
                  pl.BlockSpec(l16.shape, lambda i: (0, 0, 0, 0)),
                  pl.BlockSpec((nq, b, sl), lambda i: (0, 0, 0))],
        out_specs=[pl.BlockSpec((nq, b, jb, dl), lambda i: (0, 0, i, 0)),
                   pl.BlockSpec((nq, b, jb, dl), lambda i: (0, 0, nblk - 1 - i, 0)),
                   pl.BlockSpec((nq, b, sl), lambda i: (0, 0, 0))],
        out_shape=[jax.ShapeDtypeStruct((nq, b, nj, dl), F32), jax.ShapeDtypeStruct((nq, b, nj, dl), F32),
                   jax.ShapeDtypeStruct((nq, b, sl), F32)],
        scratch_shapes=[pltpu.VMEM((nq, b, sl), F32)],
        compiler_params=_cparams("arbitrary"),
        name="s5_scan",
    )(s, s, l16, h0)
    nh = 2
    y = pl.pallas_call(
        _s5_out_kernel,
        grid=(nq, nh, b),
        in_specs=[pl.BlockSpec((1, 1, nj, cl), lambda q, h, bi: (q, bi, 0, 0)),
                  pl.BlockSpec((1, 1, nj, dl), lambda q, h, bi: (q, bi, 0, 0)),
                  pl.BlockSpec((1, 1, nj, dl), lambda q, h, bi: (q, bi, 0, 0)),
                  pl.BlockSpec((1, cl, cl // nh), lambda q, h, bi: (q, 0, h)),
                  pl.BlockSpec((1, sl, cl // nh), lambda q, h, bi: (q, 0, h))],
        out_specs=pl.BlockSpec((1, nj, L // nh, lanes), lambda q, h, bi: (bi, 0, h, q)),
        out_shape=jax.ShapeDtypeStruct((b, nj, L, w), F32),
        compiler_params=_cparams("parallel", "parallel", "parallel"),
        name="s5_out",
    )(xs, hf, hb, mt, wc)
    return y.reshape(b, t, w), hfin


SUB_ROWS = 256


def _sub_rows(tm):
    return [slice(i, i + min(SUB_ROWS, tm)) for i in range(0, tm, min(SUB_ROWS, tm))]


def _route_logits(h, wr_ref, br_ref):
    h_hi = h.astype(BF16)
    h_lo = (h - h_hi.astype(F32)).astype(BF16)
    t = jnp.dot(h_hi, wr_ref[...], preferred_element_type=F32)
    return (t[:, :ROUTE_LANES] + t[:, ROUTE_LANES:]
            + jnp.dot(h_lo, wr_ref[:, 0:ROUTE_LANES], preferred_element_type=F32) + br_ref[...])


def _route_select(h, logits, hx_ref, rec_ref, rows):
    lane = lax.broadcasted_iota(jnp.int32, logits.shape, 1)
    big = jnp.int32(ROUTE_LANES)
    l1 = jnp.where(lane < MOE_GROUPS, logits, NEG_INF)
    m1 = jnp.max(l1, axis=-1, keepdims=True)
    gidx = jnp.min(jnp.where(l1 == m1, lane, big), axis=-1, keepdims=True)
    gval = 1.0 / jnp.sum(jnp.where(lane < MOE_GROUPS, jnp.exp(logits - m1), 0.0), axis=-1, keepdims=True)
    lo = MOE_GROUPS + MOE_PER_GROUP * gidx
    l2 = jnp.where((lane >= lo) & (lane < lo + MOE_PER_GROUP), logits, NEG_INF)
    v1 = jnp.max(l2, axis=-1, keepdims=True)
    i1 = jnp.min(jnp.where(l2 == v1, lane, big), axis=-1, keepdims=True)
    l2 = jnp.where(lane == i1, NEG_INF, l2)
    v2 = jnp.max(l2, axis=-1, keepdims=True)
    i2 = jnp.min(jnp.where(l2 == v2, lane, big), axis=-1, keepdims=True)
    e = jnp.exp(v2 - v1)
    wa = gval / (1.0 + e)
    wb = wa * e
    first = i1 <= i2
    w_lo, w_hi = jnp.where(first, wa, wb), jnp.where(first, wb, wa)
    e_lo = (jnp.minimum(i1, i2) - MOE_GROUPS).astype(F32)
    e_hi = (jnp.maximum(i1, i2) - MOE_GROUPS).astype(F32)
    rec = jnp.where(lane == 0, w_lo, jnp.where(lane == 1, w_hi, jnp.where(lane == 2, e_lo,
                    jnp.where(lane == 3, e_hi, 0.0))))
    d = h.shape[-1]
    hx_ref[0, rows, 0:d] = h
    hx_ref[0, rows, d:d + ROUTE_LANES] = rec
    rec_ref[0, :, rows] = rec.T[0:REC_FIELDS, :]


def _router_tables(r1_w, r1_b, r2_w, r2_b):
    d = r1_w.shape[0]
    wr = jnp.zeros((d, ROUTE_LANES), F32)
    wr = wr.at[:, :MOE_GROUPS].set(r1_w)
    wr = wr.at[:, MOE_GROUPS:MOE_GROUPS + MOE_EXPERTS].set(jnp.transpose(r2_w, (1, 0, 2)).reshape(d, MOE_EXPERTS))
    br = jnp.zeros((1, ROUTE_LANES), F32)
    br = br.at[0, :MOE_GROUPS].set(r1_b)
    br = br.at[0, MOE_GROUPS:MOE_GROUPS + MOE_EXPERTS].set(r2_b.reshape(MOE_EXPERTS))
    wr_hi = wr.astype(BF16)
    wr_lo = (wr - wr_hi.astype(F32)).astype(BF16)
    return jnp.concatenate([wr_hi, wr_lo], axis=1), br


def _merge_kernel(x_ref, ga_ref, hf_ref, hb_ref, ub_ref, ys_ref, d_ref, gw_ref, gb_ref, wo_ref, gate_ref,
                  g2_ref, sh_ref, sc_ref, wr_ref, br_ref, xo_ref, h_ref, rec_ref):
    w = ga_ref.shape[-1]
    subs = _sub_rows(x_ref.shape[1])
    y_a = [((hf_ref[0, r, :] + hb_ref[0, r, :]) * _gelu(ga_ref[0, r, :])).astype(BF16) for r in subs]
    y_s = [_gelu(ys_ref[0, r, :] + d_ref[...] * ub_ref[0, r, :]) for r in subs]
    glu = [_bdot(v, gw_ref[...]) for v in y_s]
    y_s = [(v * jax.nn.sigmoid(g + gb_ref[...])).astype(BF16) for v, g in zip(y_s, glu)]
    y = [jnp.dot(a, wo_ref[0:w, :], preferred_element_type=F32)
         + jnp.dot(s, wo_ref[w:2 * w, :], preferred_element_type=F32) for a, s in zip(y_a, y_s)]
    h = []
    for r, v in zip(subs, y):
        x = x_ref[0, r, :] + gate_ref[0] * v
        xo_ref[0, r, :] = x
        h.append(_norm_mod(x, g2_ref[...], sh_ref[0], sc_ref[0]))
    logits = [_route_logits(v, wr_ref, br_ref) for v in h]
    for r, v, lg in zip(subs, h, logits):
        _route_select(v, lg, h_ref, rec_ref, r)


def _merge(x, ga, hf, hb, ub, ys, s5_d, glu_w, glu_b, w_out, gate, g2, shift, scale, wr, br):
    b, t, d = x.shape
    w = ga.shape[-1]
    tm = min(TOKEN_TILE, t)
    tok = lambda n: pl.BlockSpec((1, tm, n), lambda bi, i: (bi, i, 0))
    row = pl.BlockSpec((1, 1, d), lambda bi, i: (bi, 0, 0))
    full = lambda shape: pl.BlockSpec(shape, lambda bi, i: (0,) * len(shape))
    return pl.pallas_call(
        _merge_kernel,
        grid=(b, t // tm),
        in_specs=[tok(d), tok(w), tok(w), tok(w), tok(w), tok(w), full((1, w)), full((w, w)), full((1, w)),
                  full((2 * w, d)), row, full((1, d)), row, row, full(wr.shape), full(br.shape)],
        out_specs=[tok(d), tok(d + ROUTE_LANES), pl.BlockSpec((1, REC_FIELDS, tm), lambda bi, i: (bi, 0, i))],
        out_shape=[jax.ShapeDtypeStruct((b, t, d), F32), jax.ShapeDtypeStruct((b, t, d + ROUTE_LANES), F32),
                   jax.ShapeDtypeStruct((b, REC_FIELDS, t), F32)],
        compiler_params=_cparams("parallel", "parallel"),
        name="merge_route",
    )(x, ga, hf, hb, ub, ys, s5_d, glu_w, glu_b, w_out, gate, g2, shift, scale, wr, br)


DMA_GROUP = 8


EXPERT_CHUNK = 256


GATHER_AHEAD = 2


def _expert_kernel(elo_ref, ehi_ref, nrows_ref, g0_ref, g1_ref, g2_ref, sprv_ref, scur_ref, hx_hbm,
                   gl_ref, ul_ref, dl_ref, gh_ref, uh_ref, dh_ref, y_hbm, xb, yb, gsem, ssem):
    t = pl.program_id(0)
    nt = pl.num_programs(0)
    n = nrows_ref[t]
    _, tm, d = yb.shape
    f = gl_ref.shape[-1]
    nxb = GATHER_AHEAD + 1

    def gather_row(iref, r, s):
        return pltpu.make_async_copy(hx_hbm.at[pl.ds(iref[0, 0, r], 1)], xb.at[s, pl.ds(r, 1)], gsem.at[s])

    def scatter_row(iref, r, s):
        return pltpu.make_async_copy(yb.at[s, pl.ds(r, 1)], y_hbm.at[pl.ds(iref[0, 0, r], 1)], ssem.at[s])

    def gather_wait(s):
        pltpu.make_async_copy(hx_hbm.at[pl.ds(0, tm)], xb.at[s], gsem.at[s]).wait()

    def scatter_wait(s):
        pltpu.make_async_copy(yb.at[s], y_hbm.at[pl.ds(0, tm)], ssem.at[s]).wait()

    def all_rows(fn):
        g = DMA_GROUP
        lax.fori_loop(0, tm // g, lambda k, c: ([fn(k * g + j) for j in range(g)], c)[1], 0)

    @pl.when(t == 0)
    def _():
        yb[...] = jnp.zeros_like(yb)
        first_trash = pltpu.make_async_copy(yb.at[0], y_hbm.at[pl.ds(y_hbm.shape[0] - 2 * tm, tm)], ssem.at[0])
        first_trash.start()
        first_trash.wait()
        all_rows(lambda r: gather_row(g0_ref, r, 0).start())
        all_rows(lambda r: gather_row(g1_ref, r, 1).start())

    s = t % 2
    o = 1 - s
    cur = t % nxb
    nx1 = (t + 1) % nxb
    nx2 = (t + 2) % nxb

    def step():
        gather_wait(cur)
        x = xb[cur, :, 0:d].astype(BF16)
        gates = (xb[cur, :, d:d + 1], xb[cur, :, d + 1:d + 2])
        nchunk = f // EXPERT_CHUNK
        per = tm // (2 * nchunk)
        hids = []
        for e, (g_ref, u_ref) in enumerate(((gl_ref, ul_ref), (gh_ref, uh_ref))):
            for k in range(nchunk):
                c = e * nchunk + k
                for r in range(c * per, (c + 1) * per):
                    gather_row(g2_ref, r, nx2).start()
                    scatter_row(sprv_ref, r, o).start()
                cs = slice(k * EXPERT_CHUNK, (k + 1) * EXPERT_CHUNK)
                hid = _silu(jnp.dot(x, g_ref[0, :, cs], preferred_element_type=F32)) * jnp.dot(
                    x, u_ref[0, :, cs], preferred_element_type=F32)
                hids.append((hid * gates[e]).astype(BF16))
        y = None
        for e, d_ref in enumerate((dl_ref, dh_ref)):
            for k in range(nchunk):
                part = jnp.dot(hids[e * nchunk + k], d_ref[0, k * EXPERT_CHUNK:(k + 1) * EXPERT_CHUNK, :],
                               preferred_element_type=F32)
                y = part if y is None else y + part

        @pl.when(t > 0)
        def _():
            scatter_wait(s)
        yb[s] = y

        @pl.when(t == nt - 1)
        def _():
            gather_wait(nx1)
            gather_wait(nx2)
            scatter_wait(o)
            all_rows(lambda r: scatter_row(scur_ref, r, s).start())
            scatter_wait(s)

    def drain():
        gather_wait(cur)
        gather_wait(nx1)
        scatter_wait(s)
        all_rows(lambda r: scatter_row(sprv_ref, r, o).start())
        scatter_wait(o)

    had_rows = nrows_ref[jnp.maximum(t - 1, 0)] > 0
    pl.when(n > 0)(step)
    pl.when((n == 0) & (t > 0) & had_rows)(drain)


def _moe_schedule(e_lo, e_hi, tm):
    n = e_lo.shape[0]
    lo = jnp.clip(e_lo.astype(jnp.int32), 0, MOE_EXPERTS - 1)
    hi = jnp.clip(e_hi.astype(jnp.int32), 0, MOE_EXPERTS - 1)
    nbk = MOE_EXPERTS * MOE_EXPERTS
    bucket = lo * MOE_EXPERTS + hi
    order = jnp.argsort(bucket, stable=True).astype(jnp.int32)
    eids = jnp.arange(MOE_EXPERTS, dtype=jnp.int32)[None, :]
    count = jnp.einsum('nl,nh->lh', (lo[:, None] == eids).astype(F32), (hi[:, None] == eids).astype(F32),
                       precision=HIGHEST).astype(jnp.int32).reshape(nbk)
    start = jnp.cumsum(count) - count
    tiles = (count + tm - 1) // tm
    tile_end = jnp.cumsum(tiles)
    nt = n // tm + MOE_PAIRS
    tix = jnp.arange(nt, dtype=jnp.int32)
    total = tile_end[-1]
    bk = jnp.sum((tile_end[None, :] <= jnp.minimum(tix, total - 1)[:, None]).astype(jnp.int32), axis=1)
    bk = jnp.clip(bk, 0, nbk - 1)
    k = tix - (tile_end[bk] - tiles[bk])
    nrows = jnp.where(tix < total, jnp.clip(count[bk] - k * tm, 0, tm), 0).astype(jnp.int32)
    r = jnp.arange(tm, dtype=jnp.int32)[None, :]
    pos = jnp.clip(start[bk][:, None] + k[:, None] * tm + r, 0, n - 1)
    gidx = order[pos]
    sidx = jnp.where(r < nrows[:, None], gidx, n + (tix % 2)[:, None] * tm + r)
    sprev = jnp.concatenate([n + tm + r, sidx[:-1]], axis=0)
    shape = (nt, 1, tm)
    return (bk // MOE_EXPERTS, bk % MOE_EXPERTS, nrows, gidx.reshape(shape), sprev.reshape(shape),
            sidx.reshape(shape))


def _experts(hx, e_lo, e_hi, w_gate, w_up, w_down):
    n, dx = hx.shape
    d = dx - ROUTE_LANES
    f = w_gate.shape[-1]
    tm = EXPERT_TILE
    elo, ehi, nrows, gidx, sprev, sidx = _moe_schedule(e_lo, e_hi, tm)
    nt = nrows.shape[0]
    wspec = lambda shape, which: pl.BlockSpec(
        (1,) + shape, (lambda t, elo, ehi, nr: (elo[t], 0, 0)) if which == 0 else (lambda t, elo, ehi, nr: (ehi[t], 0, 0)))
    ispec = lambda m: pl.BlockSpec((1, 1, tm), m, memory_space=pltpu.SMEM)
    gs = pltpu.PrefetchScalarGridSpec(
        num_scalar_prefetch=3,
        grid=(nt,),
        in_specs=[ispec(lambda t, *_: (t, 0, 0)), ispec(lambda t, *_: (jnp.minimum(t + 1, nt - 1), 0, 0)),
                  ispec(lambda t, *_: (jnp.minimum(t + 2, nt - 1), 0, 0)),
                  ispec(lambda t, *_: (t, 0, 0)), ispec(lambda t, *_: (t, 0, 0)),
                  pl.BlockSpec(memory_space=pl.ANY),
                  wspec((d, f), 0), wspec((d, f), 0), wspec((f, d), 0),
                  wspec((d, f), 1), wspec((d, f), 1), wspec((f, d), 1)],
        out_specs=pl.BlockSpec(memory_space=pl.ANY),
        scratch_shapes=[pltpu.VMEM((GATHER_AHEAD + 1, tm, dx), F32), pltpu.VMEM((2, tm, d), F32),
                        pltpu.SemaphoreType.DMA((GATHER_AHEAD + 1,)), pltpu.SemaphoreType.DMA((2,))])
    return pl.pallas_call(
        _expert_kernel,
        grid_spec=gs,
        out_shape=jax.ShapeDtypeStruct((n + 2 * tm, d), F32),
        compiler_params=_cparams("arbitrary"),
        name="experts",
    )(elo, ehi, nrows, gidx, gidx, gidx, sprev, sidx, hx, w_gate, w_up, w_down, w_gate, w_up, w_down)


def _qkv_kernel(x_ref, y_ref, gate_ref, g_ref, sh_ref, sc_ref, w_ref, xo_ref, q_ref, k_ref, v_ref, *, qscale):
    x = x_ref[0] + gate_ref[0] * y_ref[...]
    xo_ref[0] = x
    h = _norm_mod(x, g_ref[...], sh_ref[0], sc_ref[0])
    r = _bdot(h, w_ref[...])
    d = x.shape[-1]
    npair = q_ref.shape[1]
    lanes = q_ref.shape[-1]
    for p in range(npair):
        q_ref[0, p] = (r[:, p * lanes:(p + 1) * lanes] * qscale).astype(BF16)
        k_ref[0, p] = r[:, d + p * lanes:d + (p + 1) * lanes].astype(BF16)
        v_ref[0, p] = r[:, 2 * d + p * lanes:2 * d + (p + 1) * lanes].astype(BF16)


def _flat_rows(row0, t, tm, d):
    return pl.BlockSpec((tm, d), lambda bi, i: (row0 // tm + bi * (t // tm) + i, 0))


def _qkv(x, y, row0, gate, g, shift, scale, w_qkv):
    b, t, d = x.shape
    tm = min(TOKEN_TILE, t)
    assert row0 % tm == 0
    npair = NA_HEADS // 2
    lanes = d // npair
    tok = pl.BlockSpec((1, tm, d), lambda bi, i: (bi, i, 0))
    row = pl.BlockSpec((1, 1, d), lambda bi, i: (bi, 0, 0))
    hp = pl.BlockSpec((1, npair, tm, lanes), lambda bi, i: (bi, 0, i, 0))
    hps = jax.ShapeDtypeStruct((b, npair, t, lanes), BF16)
    return pl.pallas_call(
        functools.partial(_qkv_kernel, qscale=float((d // NA_HEADS) ** -0.5)),
        grid=(b, t // tm),
        in_specs=[tok, _flat_rows(row0, t, tm, d), row, pl.BlockSpec((1, d), lambda bi, i: (0, 0)), row, row,
                  pl.BlockSpec(w_qkv.shape, lambda bi, i: (0, 0))],
        out_specs=[tok, hp, hp, hp],
        out_shape=[jax.ShapeDtypeStruct((b, t, d), F32), hps, hps, hps],
        compiler_params=_cparams("parallel", "parallel"),
        name="qkv",
    )(x, y, gate, g, shift, scale, w_qkv)


def _na_bias_tables(rpb):
    h = rpb.shape[0]
    col = jnp.arange(GRID_W)
    c_start = jnp.clip(col - NA_KC // 2, 0, GRID_W - NA_KC)
    col_ok = (col[None, :] >= c_start[:, None]) & (col[None, :] < c_start[:, None] + NA_KC)
    dc_idx = jnp.clip(col[None, :] - col[:, None] + NA_KC - 1, 0, 2 * NA_KC - 2)
    rc = jnp.where(col_ok[None, None], rpb[:, :, dc_idx].astype(F32), NEG_INF)
    dv = jnp.arange(NA_KR)[:, None]
    kr = jnp.arange(NA_KR)[None, :]
    t = rc[:, kr - dv + NA_KR - 1]
    t = t.reshape(h // 2, 2, NA_KR, NA_KR, GRID_W, GRID_W)
    return jnp.transpose(t, (0, 2, 1, 4, 3, 5)).reshape(h // 2, NA_KR, 2 * GRID_W, NA_KR * GRID_W)


def _attn_kernel(q_ref, kp_ref, kc_ref, kn_ref, vp_ref, vc_ref, vn_ref, kx_ref, vx_ref, bias_ref, o_ref,
                 kbuf, vbuf, q2_ref, sc_ref, pl_ref, pc_ref, li_ref, *, rows):
    i = pl.program_id(2)
    blk = kc_ref.shape[2]
    kbuf[0:blk, :] = kp_ref[0, 0]
    kbuf[blk:2 * blk, :] = kc_ref[0, 0]
    kbuf[2 * blk:3 * blk, :] = kn_ref[0, 0]
    vbuf[0:blk, :] = vp_ref[0, 0]
    vbuf[blk:2 * blk, :] = vc_ref[0, 0]
    vbuf[2 * blk:3 * blk, :] = vn_ref[0, 0]
    lanes = q_ref.shape[-1]
    w2 = 2 * GRID_W
    lane = lax.broadcasted_iota(jnp.int32, (GRID_W, lanes), 1)
    first = lane < lanes // 2
    nt_dims = (((1,), (1,)), ((), ()))
    for rho in range(NA_KR):
        q = q_ref[0, 0, rho * GRID_W:(rho + 1) * GRID_W, :]
        q2_ref[rho * w2:rho * w2 + GRID_W, :] = jnp.where(first, q, jnp.zeros_like(q))
        q2_ref[rho * w2 + GRID_W:(rho + 1) * w2, :] = jnp.where(first, jnp.zeros_like(q), q)
    sc_ref[...] = lax.dot_general(q2_ref[...], kx_ref[0, 0], nt_dims, preferred_element_type=F32)

    def window(rho):
        r = NA_KR * i + rho
        rs = jnp.clip(r - NA_KR // 2, 0, rows - NA_KR)
        return pl.multiple_of((rs - NA_KR * i + NA_KR) * GRID_W, GRID_W), r - rs

    for rho in range(NA_KR):
        wstart, dvar = window(rho)
        sl = slice(rho * w2, (rho + 1) * w2)
        s_loc = lax.dot_general(q2_ref[sl, :], kbuf[pl.ds(wstart, blk), :], nt_dims,
                                preferred_element_type=F32) + bias_ref[0, dvar]
        s_ctx = sc_ref[sl, :]
        m = jnp.maximum(jnp.max(s_loc, axis=-1, keepdims=True), jnp.max(s_ctx, axis=-1, keepdims=True))
        p_loc = jnp.exp(s_loc - m)
        p_ctx = jnp.exp(s_ctx - m)
        den = jnp.sum(p_loc, axis=-1, keepdims=True) + jnp.sum(p_ctx, axis=-1, keepdims=True)
        pl_ref[rho] = p_loc.astype(BF16)
        pc_ref[sl, :] = p_ctx.astype(BF16)
        li_ref[sl, :] = jnp.broadcast_to(1.0 / den, (w2, lanes))

    o_ctx = jnp.dot(pc_ref[...], vx_ref[0, 0], preferred_element_type=F32)
    for rho in range(NA_KR):
        wstart, _ = window(rho)
        sl = slice(rho * w2, (rho + 1) * w2)
        o = jnp.dot(pl_ref[rho], vbuf[pl.ds(wstart, blk), :], preferred_element_type=F32)
        o = (o + o_ctx[sl, :]) * li_ref[sl, :]
        o_ref[0, rho * GRID_W:(rho + 1) * GRID_W, :] = jnp.where(first, o[:GRID_W], o[GRID_W:]).astype(o_ref.dtype)


def _attention(q, k, v, kx, vx, bias):
    b, npair, t, lanes = q.shape
    c = kx.shape[2]
    blk = NA_KR * GRID_W
    nb = t // blk
    rows = t // GRID_W
    cur = lambda p, bi, i: (bi, p, i, 0)
    prv = lambda p, bi, i: (bi, p, jnp.maximum(i - 1, 0), 0)
    nxt = lambda p, bi, i: (bi, p, jnp.minimum(i + 1, nb - 1), 0)
    tb = lambda m: pl.BlockSpec((1, 1, blk, lanes), m)
    cx = pl.BlockSpec((1, 1, c, lanes), lambda p, bi, i: (bi, p, 0, 0))
    return pl.pallas_call(
        functools.partial(_attn_kernel, rows=rows),
        grid=(npair, b, nb),
        in_specs=[tb(cur), tb(prv), tb(cur), tb(nxt), tb(prv), tb(cur), tb(nxt), cx, cx,
                  pl.BlockSpec((1,) + bias.shape[1:], lambda p, bi, i: (p, 0, 0, 0))],
        out_specs=pl.BlockSpec((1, blk, lanes), lambda p, bi, i: (bi, i, p)),
        out_shape=jax.ShapeDtypeStruct((b, t, npair * lanes), BF16),
        scratch_shapes=[pltpu.VMEM((3 * blk, lanes), BF16), pltpu.VMEM((3 * blk, lanes), BF16),
                        pltpu.VMEM((2 * blk, lanes), BF16), pltpu.VMEM((2 * blk, c), F32),
                        pltpu.VMEM((NA_KR, 2 * GRID_W, blk), BF16), pltpu.VMEM((2 * blk, c), BF16),
                        pltpu.VMEM((2 * blk, lanes), F32)],
        compiler_params=_cparams("parallel", "parallel", "parallel"),
        name="na_attention",
    )(q, k, k, k, v, v, v, kx, vx, bias)


def _oproj_kernel(x_ref, o_ref, wo_ref, gate_ref, g2_ref, sh_ref, sc_ref, wr_ref, br_ref, xo_ref, h_ref, rec_ref):
    subs = _sub_rows(x_ref.shape[1])
    y = [jnp.dot(o_ref[0, r, :], wo_ref[...], preferred_element_type=F32) for r in subs]
    h = []
    for r, v in zip(subs, y):
        x = x_ref[0, r, :] + gate_ref[0] * v
        xo_ref[0, r, :] = x
        h.append(_norm_mod(x, g2_ref[...], sh_ref[0], sc_ref[0]))
    logits = [_route_logits(v, wr_ref, br_ref) for v in h]
    for r, v, lg in zip(subs, h, logits):
        _route_select(v, lg, h_ref, rec_ref, r)


def _oproj(x, o, w_out, gate, g2, shift, scale, wr, br):
    b, t, d = x.shape
    tm = min(TOKEN_TILE, t)
    tok = lambda n: pl.BlockSpec((1, tm, n), lambda bi, i: (bi, i, 0))
    row = pl.BlockSpec((1, 1, d), lambda bi, i: (bi, 0, 0))
    full = lambda shape: pl.BlockSpec(shape, lambda bi, i: (0,) * len(shape))
    return pl.pallas_call(
        _oproj_kernel,
        grid=(b, t // tm),
        in_specs=[tok(d), tok(d), full((d, d)), row, full((1, d)), row, row, full(wr.shape), full(br.shape)],
        out_specs=[tok(d), tok(d + ROUTE_LANES), pl.BlockSpec((1, REC_FIELDS, tm), lambda bi, i: (bi, 0, i))],
        out_shape=[jax.ShapeDtypeStruct((b, t, d), F32), jax.ShapeDtypeStruct((b, t, d + ROUTE_LANES), F32),
                   jax.ShapeDtypeStruct((b, REC_FIELDS, t), F32)],
        compiler_params=_cparams("parallel", "parallel"),
        name="oproj_route",
    )(x, o, w_out, gate, g2, shift, scale, wr, br)


def _final_kernel(x_ref, y_ref, gate_ref, g_ref, o_ref):
    x = x_ref[0] + gate_ref[0] * y_ref[...]
    ms = jnp.mean(x * x, axis=-1, keepdims=True)
    o_ref[0] = (x * lax.rsqrt(ms + RMS_EPS)) * g_ref[...]


def _final(x, y, gate, g):
    b, t, d = x.shape
    tm = min(TOKEN_TILE, t)
    tok = pl.BlockSpec((1, tm, d), lambda bi, i: (bi, i, 0))
    return pl.pallas_call(
        _final_kernel,
        grid=(b, t // tm),
        in_specs=[tok, _flat_rows(0, t, tm, d), pl.BlockSpec((1, 1, d), lambda bi, i: (bi, 0, 0)),
                  pl.BlockSpec((1, d), lambda bi, i: (0, 0))],
        out_specs=tok,
        out_shape=jax.ShapeDtypeStruct((b, t, d), F32),
        compiler_params=_cparams("parallel", "parallel"),
        name="final_norm",
    )(x, y, gate, g)


def kernel(x, c, ctx, c_ctx, ada_w, ada_b, norm1_g, norm2_g, rec_w_in, rec_conv_w, rec_conv_b, lru_wa, lru_ba, lru_wx, lru_bx, lru_lambda, s5_a_re, s5_a_im, s5_log_dt, s5_b_re, s5_b_im, s5_c_re, s5_c_im, s5_d, s5_glu_w, s5_glu_b, rec_w_out, na_w_qkv, na_w_out, na_rpb, moe_r1_w, moe_r1_b, moe_r2_w, moe_r2_b, moe_w_gate, moe_w_up, moe_w_down, final_norm_g):
    b, t, d = x.shape
    tc = ctx.shape[1]
    assert ada_w.shape[0] == 2, "layer 0 recurrent mixer, layer 1 neighbourhood attention"
    w = rec_w_in.shape[-1] // 3

    rpad = -(b + 1) % 8
    cc = jnp.concatenate([c, c_ctx[None], jnp.zeros((rpad, d), F32)], axis=0)
    mod = _ada_mod(cc, ada_w, ada_b)

    def mods(layer, ctx_rows):
        rows = jnp.broadcast_to(mod[layer, b:b + 1], (b, 6 * d)) if ctx_rows else mod[layer, :b]
        return [rows[:, j * d:(j + 1) * d].reshape(b, 1, d) for j in range(6)]

    row = lambda v: v.reshape(1, -1)

    w_in = rec_w_in[0].astype(BF16)
    lru = [(_block_diag(lru_wa[0, dr]).astype(BF16), row(lru_ba[0, dr]), _block_diag(lru_wx[0, dr]).astype(BF16),
            row(lru_bx[0, dr]), row(lru_lambda[0, dr])) for dr in (0, 1)]
    s5t = _s5_tables(s5_a_re[0], s5_a_im[0], s5_log_dt[0], s5_b_re[0], s5_b_im[0], s5_c_re[0], s5_c_im[0])
    glu_w = s5_glu_w[0].astype(BF16)
    w_out0 = rec_w_out[0].astype(BF16)
    wr0, br0 = _router_tables(moe_r1_w[0], moe_r1_b[0], moe_r2_w[0], moe_r2_b[0])

    def mixer0(xs, m, h0_lru, h0_s5):
        xa, ga, ub = _inproj(xs, row(norm1_g[0]), m[0], m[1], w_in)
        hf = _lru_dir(xa, rec_conv_w[0], row(rec_conv_b[0]), *lru[0], h0_lru[0], False)
        hb = _lru_dir(xa, rec_conv_w[0], row(rec_conv_b[0]), *lru[1], h0_lru[1], True)
        ys, s5_fin = _s5(ub, s5t, h0_s5)
        x_mid, h2, rec = _merge(xs, ga, hf, hb, ub, ys, row(s5_d[0]), glu_w, row(s5_glu_b[0]), w_out0, m[2],
                                row(norm2_g[0]), m[3], m[4], wr0, br0)
        return x_mid, h2, rec, (hf[:, -1:], hb[:, :1]), s5_fin

    zl = jnp.zeros((b, 1, w), F32)
    zs = jnp.zeros((s5t[0].shape[0], b, s5t[0].shape[-1]), F32)
    mc0, ml0 = mods(0, True), mods(0, False)
    xc_mid, hc2, recc, lru_fin, s5_fin = mixer0(ctx, mc0, (zl, zl), zs)
    xl_mid, hl2, recl, _, _ = mixer0(x, ml0, lru_fin, s5_fin)

    h_all = jnp.concatenate([hl2.reshape(b * t, -1), hc2.reshape(b * tc, -1)], axis=0)
    field = lambda k: jnp.concatenate([recl[:, k].reshape(b * t), recc[:, k].reshape(b * tc)])
    y_all = _experts(h_all, field(2), field(3), moe_w_gate[0].astype(BF16), moe_w_up[0].astype(BF16),
                     moe_w_down[0].astype(BF16))

    w_qkv = na_w_qkv[0].astype(BF16)
    mc1, ml1 = mods(1, True), mods(1, False)
    _, _, kx, vx = _qkv(xc_mid, y_all, b * t, mc0[5], row(norm1_g[1]), mc1[0], mc1[1], w_qkv)
    x1, q, k, v = _qkv(xl_mid, y_all, 0, ml0[5], row(norm1_g[1]), ml1[0], ml1[1], w_qkv)
    o = _attention(q, k, v, kx, vx, _na_bias_tables(na_rpb[0]))
    wr1, br1 = _router_tables(moe_r1_w[1], moe_r1_b[1], moe_r2_w[1], moe_r2_b[1])
    x1_mid, h2, rec = _oproj(x1, o, na_w_out[0].astype(BF16), ml1[2], row(norm2_g[1]), ml1[3], ml1[4], wr1, br1)
    y1 = _experts(h2.reshape(b * t, -1), rec[:, 2].reshape(b * t), rec[:, 3].reshape(b * t), moe_w_gate[1].astype(BF16),
                  moe_w_up[1].astype(BF16), moe_w_down[1].astype(BF16))
    return _final(x1_mid, y1, ml1[5], row(final_norm_g))
```

```python
import functools

import jax
import jax.numpy as jnp
from jax import lax
from jax.experimental import pallas as pl
from jax.experimental.pallas import tpu as pltpu

F32 = jnp.float32
BF16 = jnp.bfloat16
HIGHEST = lax.Precision.HIGHEST

RMS_EPS = 1e-6
GRID_W = 64
LRU_C = 8.0
S5_GROUP = 16
S5_CHUNK = 16
NA_HEADS = 16
NA_KR = 8
NA_KC = 16
NEG_INF = -1e30
MOE_GROUPS = 4
MOE_PER_GROUP = 8
MOE_EXPERTS = MOE_GROUPS * MOE_PER_GROUP
MOE_PAIRS = MOE_GROUPS * (MOE_PER_GROUP * (MOE_PER_GROUP - 1) // 2)
EXPERT_TILE = 256
ROUTE_LANES = 128
REC_FIELDS = 8
TOKEN_TILE = 512
WIDE_TILE = 1024
VMEM_LIMIT = 56 * 1024 * 1024


def _cparams(*sem):
    return pltpu.CompilerParams(dimension_semantics=sem, vmem_limit_bytes=VMEM_LIMIT)


def _norm_mod(x, g, shift, scale):
    ms = jnp.mean(x * x, axis=-1, keepdims=True)
    return (x * lax.rsqrt(ms + RMS_EPS)) * g * (1.0 + scale) + shift


def _silu(x):
    return x * jax.nn.sigmoid(x)


def _gelu(x):
    return jax.nn.gelu(x, approximate=True)


def _bdot(a, b):
    return jnp.dot(a.astype(BF16), b, preferred_element_type=F32)


def _ada_kernel(c_ref, w_ref, b_ref, o_ref):
    o_ref[0] = jnp.dot(_silu(c_ref[...]), w_ref[0], preferred_element_type=F32,
                       precision=HIGHEST) + b_ref[0]


def _ada_mod(cc, ada_w, ada_b):
    n_layers, d, d6 = ada_w.shape
    r = cc.shape[0]
    return pl.pallas_call(
        _ada_kernel,
        grid=(n_layers, d6 // d),
        in_specs=[pl.BlockSpec((r, d), lambda l, j: (0, 0)),
                  pl.BlockSpec((1, d, d), lambda l, j: (l, 0, j)),
                  pl.BlockSpec((1, 1, d), lambda l, j: (l, 0, j))],
        out_specs=pl.BlockSpec((1, r, d), lambda l, j: (l, 0, j)),
        out_shape=jax.ShapeDtypeStruct((n_layers, r, d6), F32),
        compiler_params=_cparams("arbitrary", "arbitrary"),
        name="ada_mod",
    )(cc, ada_w, ada_b.reshape(n_layers, 1, d6))


def _inproj_kernel(x_ref, g_ref, sh_ref, sc_ref, w_ref, xa_ref, ga_ref, ub_ref):
    h = _norm_mod(x_ref[0], g_ref[...], sh_ref[0], sc_ref[0])
    r = _bdot(h, w_ref[...])
    w = xa_ref.shape[-1]
    xa_ref[0] = r[:, :w]
    ga_ref[0] = r[:, w:2 * w]
    ub_ref[0] = r[:, 2 * w:]


def _inproj(x, g, shift, scale, w_in):
    b, t, d = x.shape
    w = w_in.shape[1] // 3
    tm = min(WIDE_TILE, t)
    row = pl.BlockSpec((1, 1, d), lambda bi, i: (bi, 0, 0))
    out = pl.BlockSpec((1, tm, w), lambda bi, i: (bi, i, 0))
    return pl.pallas_call(
        _inproj_kernel,
        grid=(b, t // tm),
        in_specs=[pl.BlockSpec((1, tm, d), lambda bi, i: (bi, i, 0)),
                  pl.BlockSpec((1, d), lambda bi, i: (0, 0)), row, row,
                  pl.BlockSpec(w_in.shape, lambda bi, i: (0, 0))],
        out_specs=[out, out, out],
        out_shape=[jax.ShapeDtypeStruct((b, t, w), F32)] * 3,
        compiler_params=_cparams("parallel", "parallel"),
        name="inproj",
    )(x, g, shift, scale, w_in)


def _lru_kernel(xc_ref, xp_ref, xn_ref, cw_ref, cb_ref, wa_ref, ba_ref, wx_ref, bx_ref, lam_ref,
                h0_ref, o_ref, ext_ref, a_ref, b_ref, car_ref, *, reverse, nt, tt):
    i = pl.program_id(1)
    ti = (nt - 1 - i) if reverse else i
    w = o_ref.shape[-1]

    @pl.when(i == 0)
    def _():
        car_ref[...] = h0_ref[0]

    ext_ref[0:8, :] = jnp.where(ti == 0, 0.0, xp_ref[0])
    ext_ref[8:8 + tt, :] = xc_ref[0]
    ext_ref[8 + tt:16 + tt, :] = jnp.where(ti == nt - 1, 0.0, xn_ref[0])
    cw = cw_ref[...]
    u = (ext_ref[6:6 + tt, :] * cw[0:1] + ext_ref[7:7 + tt, :] * cw[1:2]
         + ext_ref[8:8 + tt, :] * cw[2:3] + ext_ref[9:9 + tt, :] * cw[3:4]) + cb_ref[...]
    r = jax.nn.sigmoid(_bdot(u, wa_ref[...]) + ba_ref[...])
    ig = jax.nn.sigmoid(_bdot(u, wx_ref[...]) + bx_ref[...])
    log_a = (-LRU_C) * r * jax.nn.softplus(-lam_ref[...])
    a = jnp.exp(log_a)
    a_ref[...] = a
    b_ref[...] = jnp.sqrt(-jnp.tanh(log_a) * (1.0 + a * a)) * (ig * u)

    nsl = tt // 8
    row = lax.broadcasted_iota(jnp.int32, (8, w), 0)

    def slab(s, carry):
        off = pl.multiple_of(((nsl - 1 - s) if reverse else s) * 8, 8)
        a = a_ref[pl.ds(off, 8), :]
        bb = b_ref[pl.ds(off, 8), :]
        for k in (1, 2, 4):
            valid = (row < 8 - k) if reverse else (row >= k)
            sh = (8 - k) if reverse else k
            a_s = jnp.where(valid, pltpu.roll(a, sh, 0), 1.0)
            b_s = jnp.where(valid, pltpu.roll(bb, sh, 0), 0.0)
            bb = bb + a * b_s
            a = a * a_s
        h = bb + a * carry
        o_ref[0, pl.ds(off, 8), :] = h
        return h[0:1] if reverse else h[7:8]

    car_ref[...] = lax.fori_loop(0, nsl, slab, car_ref[...])


def _lru_dir(xa, conv_w, conv_b, wa_bd, ba, wx_bd, bx, lam, h0, reverse):
    b, t, w = xa.shape
    tt = min(WIDE_TILE, t)
    nt = t // tt
    hb = tt // 8
    tile = (lambda i: nt - 1 - i) if reverse else (lambda i: i)
    full = lambda shape: pl.BlockSpec(shape, lambda bi, i: (0,) * len(shape))
    return pl.pallas_call(
        functools.partial(_lru_kernel, reverse=reverse, nt=nt, tt=tt),
        grid=(b, nt),
        in_specs=[pl.BlockSpec((1, tt, w), lambda bi, i: (bi, tile(i), 0)),
                  pl.BlockSpec((1, 8, w), lambda bi, i: (bi, jnp.maximum(tile(i) * hb - 1, 0), 0)),
                  pl.BlockSpec((1, 8, w), lambda bi, i: (bi, jnp.minimum((tile(i) + 1) * hb, t // 8 - 1), 0)),
                  full(conv_w.shape), full((1, w)), full((w, w)), full((1, w)), full((w, w)),
                  full((1, w)), full((1, w)),
                  pl.BlockSpec((1, 1, w), lambda bi, i: (bi, 0, 0))],
        out_specs=pl.BlockSpec((1, tt, w), lambda bi, i: (bi, tile(i), 0)),
        out_shape=jax.ShapeDtypeStruct((b, t, w), F32),
        scratch_shapes=[pltpu.VMEM((tt + 16, w), F32), pltpu.VMEM((tt, w), F32),
                        pltpu.VMEM((tt, w), F32), pltpu.VMEM((1, w), F32)],
        compiler_params=_cparams("parallel", "arbitrary"),
        name="lru_bwd" if reverse else "lru_fwd",
    )(xa, xa, xa, conv_w, conv_b, wa_bd, ba, wx_bd, bx, lam, h0)


def _block_diag(w):
    h, d, _ = w.shape
    eye = jnp.eye(h, dtype=w.dtype)
    return (eye[:, None, :, None] * w[:, :, None, :]).reshape(h * d, h * d)


def _s5_tables(a_re, a_im, log_dt, b_re, b_im, c_re, c_im):
    L = S5_CHUNK
    g, n = a_re.shape[1], a_re.shape[2]
    p = b_re.shape[-1]
    f = lambda x: x.astype(F32)
    a_re, a_im, b_re, b_im, c_re, c_im = map(f, (a_re, a_im, b_re, b_im, c_re, c_im))
    dt = jnp.exp(f(log_dt))[..., None]
    mag = jnp.exp(a_re * dt)
    lb_re, lb_im = mag * jnp.cos(a_im * dt), mag * jnp.sin(a_im * dt)
    den = a_re * a_re + a_im * a_im
    q_re = ((lb_re - 1.0) * a_re + lb_im * a_im) / den
    q_im = (lb_im * a_re - (lb_re - 1.0) * a_im) / den
    bb_re = q_re[..., None] * b_re - q_im[..., None] * b_im
    bb_im = q_re[..., None] * b_im + q_im[..., None] * b_re
    pw_re, pw_im = [jnp.ones_like(lb_re)], [jnp.zeros_like(lb_im)]
    for _ in range(L):
        r_, i_ = pw_re[-1], pw_im[-1]
        pw_re.append(r_ * lb_re - i_ * lb_im)
        pw_im.append(r_ * lb_im + i_ * lb_re)
    pw_re, pw_im = jnp.stack(pw_re, 1), jnp.stack(pw_im, 1)
    es = functools.partial(jnp.einsum, precision=HIGHEST)
    kf = jnp.arange(L - 1, -1, -1)
    kb = jnp.arange(L)
    inc = []
    for d_, ks in ((0, kf), (1, kb)):
        pr, pi = pw_re[d_][ks], pw_im[d_][ks]
        inc.append((pr[..., None] * bb_re[d_] - pi[..., None] * bb_im[d_],
                    pr[..., None] * bb_im[d_] + pi[..., None] * bb_re[d_]))
    gq = 128 // p
    nq = g // gq

    def quad_block_diag(a, outer, rpg, cpg):
        cols = a.shape[-1]
        full = jnp.broadcast_to(a[:, :, None], (nq, outer, gq, rpg, cols)).reshape(nq, outer * gq * rpg, cols)
        rg = (lax.broadcasted_iota(jnp.int32, full.shape, 1) // rpg) % gq
        cg = (lax.broadcasted_iota(jnp.int32, full.shape, 2) // cpg) % gq
        return jnp.where(rg == cg, full, 0.0).astype(BF16)

    inc_all = jnp.stack([jnp.stack(inc[0]), jnp.stack(inc[1])]).reshape(2, 2, L, nq, gq, n, p)
    wb = jnp.transpose(inc_all, (3, 2, 6, 0, 1, 4, 5)).reshape(nq, L, p, 4 * gq * n)
    wb = quad_block_diag(wb, L, p, n)
    rd = []
    for d_, ks in ((0, jnp.arange(1, L + 1)), (1, jnp.arange(L, 0, -1))):
        pr, pi = pw_re[d_][ks], pw_im[d_][ks]
        cl_re = c_re[d_][None] * pr[:, :, None, :] - c_im[d_][None] * pi[:, :, None, :]
        cl_im = c_re[d_][None] * pi[:, :, None, :] + c_im[d_][None] * pr[:, :, None, :]
        rd.append((cl_re, -cl_im))
    rd_all = jnp.stack([jnp.stack(rd[0]), jnp.stack(rd[1])]).reshape(2, 2, L, nq, gq, p, n)
    wc = jnp.transpose(rd_all, (3, 0, 1, 6, 2, 4, 5)).reshape(nq, 4, n, L * gq * p)
    wc = quad_block_diag(wc, 4, n, p)
    ker = []
    for d_ in (0, 1):
        pr, pi = pw_re[d_][:L], pw_im[d_][:L]
        cl_re = c_re[d_][None] * pr[:, :, None, :] - c_im[d_][None] * pi[:, :, None, :]
        cl_im = c_re[d_][None] * pi[:, :, None, :] + c_im[d_][None] * pr[:, :, None, :]
        ker.append(es('kgpn,gnq->kgpq', cl_re, bb_re[d_]) - es('kgpn,gnq->kgpq', cl_im, bb_im[d_]))
    s_i = jnp.arange(L)[:, None]
    t_i = jnp.arange(L)[None, :]
    kf_t = ker[0][jnp.clip(t_i - s_i, 0, L - 1)]
    kb_t = ker[1][jnp.clip(s_i - t_i, 0, L - 1)]
    m = (jnp.where((s_i <= t_i)[:, :, None, None, None], kf_t, 0.0)
         + jnp.where((s_i >= t_i)[:, :, None, None, None], kb_t, 0.0))
    mt = jnp.transpose(m.reshape(L, L, nq, gq, p, p), (2, 0, 5, 1, 3, 4)).reshape(nq, L, p, L * gq * p)
    mt = quad_block_diag(mt, L, p, p)
    l16 = jnp.stack([jnp.stack([pw_re[d_][L], pw_im[d_][L]]) for d_ in (0, 1)])
    l16 = l16.reshape(2, 2, nq, gq * n)
    return wb, mt, wc, l16


def _s5_inc_kernel(x_ref, wb_ref, xs_ref, s_ref):
    lanes = x_ref.shape[-1]
    nj = xs_ref.shape[2]
    for tau in range(S5_CHUNK):
        xs_ref[0, 0, :, tau * lanes:(tau + 1) * lanes] = x_ref[0, pl.ds(tau, nj, stride=S5_CHUNK), :].astype(BF16)
    s_ref[0, 0] = jnp.dot(xs_ref[0, 0], wb_ref[0], preferred_element_type=F32)


def _s5_scan_kernel(sf_ref, sb_ref, l16_ref, h0_ref, hf_ref, hb_ref, hfin_ref, st_ref, *, jb):
    i = pl.program_id(0)
    nq = st_ref.shape[0]
    dl = sf_ref.shape[-1]
    w = dl // 2

    @pl.when(i == 0)
    def _():
        st_ref[...] = h0_ref[...]

    for jj in range(jb):
        for q in range(nq):
            for d, (src, dst, row) in enumerate(((sf_ref, hf_ref, jj), (sb_ref, hb_ref, jb - 1 - jj))):
                s = src[q, :, row, :]
                h = st_ref[q, :, d * dl:(d + 1) * dl]
                dst[q, :, row, :] = h
                lr, li = l16_ref[d, 0, q:q + 1, :], l16_ref[d, 1, q:q + 1, :]
                hr, hi = h[:, :w], h[:, w:]
                st_ref[q, :, d * dl:(d + 1) * dl] = jnp.concatenate(
                    [lr * hr - li * hi + s[:, :w], lr * hi + li * hr + s[:, w:]], axis=-1)

    @pl.when(i == pl.num_programs(0) - 1)
    def _():
        hfin_ref[...] = st_ref[...]


def _s5_out_kernel(xs_ref, hf_ref, hb_ref, mt_ref, wc_ref, y_ref):
    dl = hf_ref.shape[-1]
    y = (jnp.dot(xs_ref[0, 0], mt_ref[0], preferred_element_type=F32)
         + _bdot(hf_ref[0, 0], wc_ref[0, 0:dl, :]) + _bdot(hb_ref[0, 0], wc_ref[0, dl:2 * dl, :]))
    lanes = y_ref.shape[-1]
    for k in range(y_ref.shape[2]):
        y_ref[0, :, k, :] = y[:, k * lanes:(k + 1) * lanes]


def _s5(ub, tables, h0):
    wb, mt, wc, l16 = tables
    b, t, w = ub.shape
    L = S5_CHUNK
    nj = t // L
    nq = wb.shape[0]
    lanes = w // nq
    cl = L * lanes
    sl = wb.shape[-1]
    xs, s = pl.pallas_call(
        _s5_inc_kernel,
        grid=(nq, b),
        in_specs=[pl.BlockSpec((1, t, lanes), lambda q, bi: (bi, 0, q)),
                  pl.BlockSpec((1, cl, sl), lambda q, bi: (q, 0, 0))],
        out_specs=[pl.BlockSpec((1, 1, nj, cl), lambda q, bi: (q, bi, 0, 0)),
                   pl.BlockSpec((1, 1, nj, sl), lambda q, bi: (q, bi, 0, 0))],
        out_shape=[jax.ShapeDtypeStruct((nq, b, nj, cl), BF16), jax.ShapeDtypeStruct((nq, b, nj, sl), F32)],
        compiler_params=_cparams("parallel", "parallel"),
        name="s5_inc",
    )(ub, wb)
    jb = 8
    nblk = nj // jb
    dl = sl // 2
    hf, hb, hfin = pl.pallas_call(
        functools.partial(_s5_scan_kernel, jb=jb),
        grid=(nblk,),
        in_specs=[pl.BlockSpec((nq, b, jb, dl), lambda i: (0, 0, i, 0)),
                  pl.BlockSpec((nq, b, jb, dl), lambda i: (0, 0, nblk - 1 - i, 1)),
                  pl.BlockSpec(l16.shape, lambda i: (0, 0, 0, 0)),
                  pl.BlockSpec((nq, b, sl), lambda i: (0, 0, 0))],
        out_specs=[pl.BlockSpec((nq, b, jb, dl), lambda i: (0, 0, i, 0)),
                   pl.BlockSpec((nq, b, jb, dl), lambda i: (0, 0, nblk - 1 - i, 0)),
                   pl.BlockSpec((nq, b, sl), lambda i: (0, 0, 0))],
        out_shape=[jax.ShapeDtypeStruct((nq, b, nj, dl), F32), jax.ShapeDtypeStruct((nq, b, nj, dl), F32),
                   jax.ShapeDtypeStruct((nq, b, sl), F32)],
        scratch_shapes=[pltpu.VMEM((nq, b, sl), F32)],
        compiler_params=_cparams("arbitrary"),
        name="s5_scan",
    )(s, s, l16, h0)
    nh = 2
    y = pl.pallas_call(
        _s5_out_kernel,
        grid=(nq, nh, b),
        in_specs=[pl.BlockSpec((1, 1, nj, cl), lambda q, h, bi: (q, bi, 0, 0)),
                  pl.BlockSpec((1, 1, nj, dl), lambda q, h, bi: (q, bi, 0, 0)),
                  pl.BlockSpec((1, 1, nj, dl), lambda q, h, bi: (q, bi, 0, 0)),
                  pl.BlockSpec((1, cl, cl // nh), lambda q, h, bi: (q, 0, h)),
                  pl.BlockSpec((1, sl, cl // nh), lambda q, h, bi: (q, 0, h))],
        out_specs=pl.BlockSpec((1, nj, L // nh, lanes), lambda q, h, bi: (bi, 0, h, q)),
        out_shape=jax.ShapeDtypeStruct((b, nj, L, w), F32),
        compiler_params=_cparams("parallel", "parallel", "parallel"),
        name="s5_out",
    )(xs, hf, hb, mt, wc)
    return y.reshape(b, t, w), hfin


SUB_ROWS = 256


def _sub_rows(tm):
    return [slice(i, i + min(SUB_ROWS, tm)) for i in range(0, tm, min(SUB_ROWS, tm))]


def _route_logits(h, wr_ref, br_ref):
    h_hi = h.astype(BF16)
    h_lo = (h - h_hi.astype(F32)).astype(BF16)
    t = jnp.dot(h_hi, wr_ref[...], preferred_element_type=F32)
    return (t[:, :ROUTE_LANES] + t[:, ROUTE_LANES:]
            + jnp.dot(h_lo, wr_ref[:, 0:ROUTE_LANES], preferred_element_type=F32) + br_ref[...])


def _route_select(h, logits, hx_ref, rec_ref, rows):
    lane = lax.broadcasted_iota(jnp.int32, logits.shape, 1)
    big = jnp.int32(ROUTE_LANES)
    l1 = jnp.where(lane < MOE_GROUPS, logits, NEG_INF)
    m1 = jnp.max(l1, axis=-1, keepdims=True)
    gidx = jnp.min(jnp.where(l1 == m1, lane, big), axis=-1, keepdims=True)
    gval = 1.0 / jnp.sum(jnp.where(lane < MOE_GROUPS, jnp.exp(logits - m1), 0.0), axis=-1, keepdims=True)
    lo = MOE_GROUPS + MOE_PER_GROUP * gidx
    l2 = jnp.where((lane >= lo) & (lane < lo + MOE_PER_GROUP), logits, NEG_INF)
    v1 = jnp.max(l2, axis=-1, keepdims=True)
    i1 = jnp.min(jnp.where(l2 == v1, lane, big), axis=-1, keepdims=True)
    l2 = jnp.where(lane == i1, NEG_INF, l2)
    v2 = jnp.max(l2, axis=-1, keepdims=True)
    i2 = jnp.min(jnp.where(l2 == v2, lane, big), axis=-1, keepdims=True)
    e = jnp.exp(v2 - v1)
    wa = gval / (1.0 + e)
    wb = wa * e
    first = i1 <= i2
    w_lo, w_hi = jnp.where(first, wa, wb), jnp.where(first, wb, wa)
    e_lo = (jnp.minimum(i1, i2) - MOE_GROUPS).astype(F32)
    e_hi = (jnp.maximum(i1, i2) - MOE_GROUPS).astype(F32)
    rec = jnp.where(lane == 0, w_lo, jnp.where(lane == 1, w_hi, jnp.where(lane == 2, e_lo,
                    jnp.where(lane == 3, e_hi, 0.0))))
    d = h.shape[-1]
    hx_ref[0, rows, 0:d] = h
    hx_ref[0, rows, d:d + ROUTE_LANES] = rec
    rec_ref[0, :, rows] = rec.T[0:REC_FIELDS, :]


def _router_tables(r1_w, r1_b, r2_w, r2_b):
    d = r1_w.shape[0]
    wr = jnp.zeros((d, ROUTE_LANES), F32)
    wr = wr.at[:, :MOE_GROUPS].set(r1_w)
    wr = wr.at[:, MOE_GROUPS:MOE_GROUPS + MOE_EXPERTS].set(jnp.transpose(r2_w, (1, 0, 2)).reshape(d, MOE_EXPERTS))
    br = jnp.zeros((1, ROUTE_LANES), F32)
    br = br.at[0, :MOE_GROUPS].set(r1_b)
    br = br.at[0, MOE_GROUPS:MOE_GROUPS + MOE_EXPERTS].set(r2_b.reshape(MOE_EXPERTS))
    wr_hi = wr.astype(BF16)
    wr_lo = (wr - wr_hi.astype(F32)).astype(BF16)
    return jnp.concatenate([wr_hi, wr_lo], axis=1), br


def _merge_kernel(x_ref, ga_ref, hf_ref, hb_ref, ub_ref, ys_ref, d_ref, gw_ref, gb_ref, wo_ref, gate_ref,
                  g2_ref, sh_ref, sc_ref, wr_ref, br_ref, xo_ref, h_ref, rec_ref):
    w = ga_ref.shape[-1]
    subs = _sub_rows(x_ref.shape[1])
    y_a = [((hf_ref[0, r, :] + hb_ref[0, r, :]) * _gelu(ga_ref[0, r, :])).astype(BF16) for r in subs]
    y_s = [_gelu(ys_ref[0, r, :] + d_ref[...] * ub_ref[0, r, :]) for r in subs]
    glu = [_bdot(v, gw_ref[...]) for v in y_s]
    y_s = [(v * jax.nn.sigmoid(g + gb_ref[...])).astype(BF16) for v, g in zip(y_s, glu)]
    y = [jnp.dot(a, wo_ref[0:w, :], preferred_element_type=F32)
         + jnp.dot(s, wo_ref[w:2 * w, :], preferred_element_type=F32) for a, s in zip(y_a, y_s)]
    h = []
    for r, v in zip(subs, y):
        x = x_ref[0, r, :] + gate_ref[0] * v
        xo_ref[0, r, :] = x
        h.append(_norm_mod(x, g2_ref[...], sh_ref[0], sc_ref[0]))
    logits = [_route_logits(v, wr_ref, br_ref) for v in h]
    for r, v, lg in zip(subs, h, logits):
        _route_select(v, lg, h_ref, rec_ref, r)


def _merge(x, ga, hf, hb, ub, ys, s5_d, glu_w, glu_b, w_out, gate, g2, shift, scale, wr, br):
    b, t, d = x.shape
    w = ga.shape[-1]
    tm = min(TOKEN_TILE, t)
    tok = lambda n: pl.BlockSpec((1, tm, n), lambda bi, i: (bi, i, 0))
    row = pl.BlockSpec((1, 1, d), lambda bi, i: (bi, 0, 0))
    full = lambda shape: pl.BlockSpec(shape, lambda bi, i: (0,) * len(shape))
    return pl.pallas_call(
        _merge_kernel,
        grid=(b, t // tm),
        in_specs=[tok(d), tok(w), tok(w), tok(w), tok(w), tok(w), full((1, w)), full((w, w)), full((1, w)),
                  full((2 * w, d)), row, full((1, d)), row, row, full(wr.shape), full(br.shape)],
        out_specs=[tok(d), tok(d + ROUTE_LANES), pl.BlockSpec((1, REC_FIELDS, tm), lambda bi, i: (bi, 0, i))],
        out_shape=[jax.ShapeDtypeStruct((b, t, d), F32), jax.ShapeDtypeStruct((b, t, d + ROUTE_LANES), F32),
                   jax.ShapeDtypeStruct((b, REC_FIELDS, t), F32)],
        compiler_params=_cparams("parallel", "parallel"),
        name="merge_route",
    )(x, ga, hf, hb, ub, ys, s5_d, glu_w, glu_b, w_out, gate, g2, shift, scale, wr, br)


DMA_GROUP = 8


EXPERT_CHUNK = 256


GATHER_AHEAD = 2


def _expert_kernel(elo_ref, ehi_ref, nrows_ref, g0_ref, g1_ref, g2_ref, sprv_ref, scur_ref, hx_hbm,
                   gl_ref, ul_ref, dl_ref, gh_ref, uh_ref, dh_ref, y_hbm, xb, yb, gsem, ssem):
    t = pl.program_id(0)
    nt = pl.num_programs(0)
    n = nrows_ref[t]
    _, tm, d = yb.shape
    f = gl_ref.shape[-1]
    nxb = GATHER_AHEAD + 1

    def gather_row(iref, r, s):
        return pltpu.make_async_copy(hx_hbm.at[pl.ds(iref[0, 0, r], 1)], xb.at[s, pl.ds(r, 1)], gsem.at[s])

    def scatter_row(iref, r, s):
        return pltpu.make_async_copy(yb.at[s, pl.ds(r, 1)], y_hbm.at[pl.ds(iref[0, 0, r], 1)], ssem.at[s])

    def gather_wait(s):
        pltpu.make_async_copy(hx_hbm.at[pl.ds(0, tm)], xb.at[s], gsem.at[s]).wait()

    def scatter_wait(s):
        pltpu.make_async_copy(yb.at[s], y_hbm.at[pl.ds(0, tm)], ssem.at[s]).wait()

    def all_rows(fn):
        g = DMA_GROUP
        lax.fori_loop(0, tm // g, lambda k, c: ([fn(k * g + j) for j in range(g)], c)[1], 0)

    @pl.when(t == 0)
    def _():
        yb[...] = jnp.zeros_like(yb)
        first_trash = pltpu.make_async_copy(yb.at[0], y_hbm.at[pl.ds(y_hbm.shape[0] - 2 * tm, tm)], ssem.at[0])
        first_trash.start()
        first_trash.wait()
        all_rows(lambda r: gather_row(g0_ref, r, 0).start())
        all_rows(lambda r: gather_row(g1_ref, r, 1).start())

    s = t % 2
    o = 1 - s
    cur = t % nxb
    nx1 = (t + 1) % nxb
    nx2 = (t + 2) % nxb

    def step():
        gather_wait(cur)
        x = xb[cur, :, 0:d].astype(BF16)
        gates = (xb[cur, :, d:d + 1], xb[cur, :, d + 1:d + 2])
        nchunk = f // EXPERT_CHUNK
        per = tm // (2 * nchunk)
        hids = []
        for e, (g_ref, u_ref) in enumerate(((gl_ref, ul_ref), (gh_ref, uh_ref))):
            for k in range(nchunk):
                c = e * nchunk + k
                for r in range(c * per, (c + 1) * per):
                    gather_row(g2_ref, r, nx2).start()
                    scatter_row(sprv_ref, r, o).start()
                cs = slice(k * EXPERT_CHUNK, (k + 1) * EXPERT_CHUNK)
                hid = _silu(jnp.dot(x, g_ref[0, :, cs], preferred_element_type=F32)) * jnp.dot(
                    x, u_ref[0, :, cs], preferred_element_type=F32)
                hids.append((hid * gates[e]).astype(BF16))
        y = None
        for e, d_ref in enumerate((dl_ref, dh_ref)):
            for k in range(nchunk):
                part = jnp.dot(hids[e * nchunk + k], d_ref[0, k * EXPERT_CHUNK:(k + 1) * EXPERT_CHUNK, :],
                               preferred_element_type=F32)
                y = part if y is None else y + part

        @pl.when(t > 0)
        def _():
            scatter_wait(s)
        yb[s] = y

        @pl.when(t == nt - 1)
        def _():
            gather_wait(nx1)
            gather_wait(nx2)
            scatter_wait(o)
            all_rows(lambda r: scatter_row(scur_ref, r, s).start())
            scatter_wait(s)

    def drain():
        gather_wait(cur)
        gather_wait(nx1)
        scatter_wait(s)
        all_rows(lambda r: scatter_row(sprv_ref, r, o).start())
        scatter_wait(o)

    had_rows = nrows_ref[jnp.maximum(t - 1, 0)] > 0
    pl.when(n > 0)(step)
    pl.when((n == 0) & (t > 0) & had_rows)(drain)


def _moe_schedule(e_lo, e_hi, tm):
    n = e_lo.shape[0]
    lo = jnp.clip(e_lo.astype(jnp.int32), 0, MOE_EXPERTS - 1)
    hi = jnp.clip(e_hi.astype(jnp.int32), 0, MOE_EXPERTS - 1)
    nbk = MOE_EXPERTS * MOE_EXPERTS
    bucket = lo * MOE_EXPERTS + hi
    order = jnp.argsort(bucket, stable=True).astype(jnp.int32)
    eids = jnp.arange(MOE_EXPERTS, dtype=jnp.int32)[None, :]
    count = jnp.einsum('nl,nh->lh', (lo[:, None] == eids).astype(F32), (hi[:, None] == eids).astype(F32),
                       precision=HIGHEST).astype(jnp.int32).reshape(nbk)
    start = jnp.cumsum(count) - count
    tiles = (count + tm - 1) // tm
    tile_end = jnp.cumsum(tiles)
    nt = n // tm + MOE_PAIRS
    tix = jnp.arange(nt, dtype=jnp.int32)
    total = tile_end[-1]
    bk = jnp.sum((tile_end[None, :] <= jnp.minimum(tix, total - 1)[:, None]).astype(jnp.int32), axis=1)
    bk = jnp.clip(bk, 0, nbk - 1)
    k = tix - (tile_end[bk] - tiles[bk])
    nrows = jnp.where(tix < total, jnp.clip(count[bk] - k * tm, 0, tm), 0).astype(jnp.int32)
    r = jnp.arange(tm, dtype=jnp.int32)[None, :]
    pos = jnp.clip(start[bk][:, None] + k[:, None] * tm + r, 0, n - 1)
    gidx = order[pos]
    sidx = jnp.where(r < nrows[:, None], gidx, n + (tix % 2)[:, None] * tm + r)
    sprev = jnp.concatenate([n + tm + r, sidx[:-1]], axis=0)
    shape = (nt, 1, tm)
    return (bk // MOE_EXPERTS, bk % MOE_EXPERTS, nrows, gidx.reshape(shape), sprev.reshape(shape),
            sidx.reshape(shape))


def _experts(hx, e_lo, e_hi, w_gate, w_up, w_down):
    n, dx = hx.shape
    d = dx - ROUTE_LANES
    f = w_gate.shape[-1]
    tm = EXPERT_TILE
    elo, ehi, nrows, gidx, sprev, sidx = _moe_schedule(e_lo, e_hi, tm)
    nt = nrows.shape[0]
    wspec = lambda shape, which: pl.BlockSpec(
        (1,) + shape, (lambda t, elo, ehi, nr: (elo[t], 0, 0)) if which == 0 else (lambda t, elo, ehi, nr: (ehi[t], 0, 0)))
    ispec = lambda m: pl.BlockSpec((1, 1, tm), m, memory_space=pltpu.SMEM)
    gs = pltpu.PrefetchScalarGridSpec(
        num_scalar_prefetch=3,
        grid=(nt,),
        in_specs=[ispec(lambda t, *_: (t, 0, 0)), ispec(lambda t, *_: (jnp.minimum(t + 1, nt - 1), 0, 0)),
                  ispec(lambda t, *_: (jnp.minimum(t + 2, nt - 1), 0, 0)),
                  ispec(lambda t, *_: (t, 0, 0)), ispec(lambda t, *_: (t, 0, 0)),
                  pl.BlockSpec(memory_space=pl.ANY),
                  wspec((d, f), 0), wspec((d, f), 0), wspec((f, d), 0),
                  wspec((d, f), 1), wspec((d, f), 1), wspec((f, d), 1)],
        out_specs=pl.BlockSpec(memory_space=pl.ANY),
        scratch_shapes=[pltpu.VMEM((GATHER_AHEAD + 1, tm, dx), F32), pltpu.VMEM((2, tm, d), F32),
                        pltpu.SemaphoreType.DMA((GATHER_AHEAD + 1,)), pltpu.SemaphoreType.DMA((2,))])
    return pl.pallas_call(
        _expert_kernel,
        grid_spec=gs,
        out_shape=jax.ShapeDtypeStruct((n + 2 * tm, d), F32),
        compiler_params=_cparams("arbitrary"),
        name="experts",
    )(elo, ehi, nrows, gidx, gidx, gidx, sprev, sidx, hx, w_gate, w_up, w_down, w_gate, w_up, w_down)


def _qkv_kernel(x_ref, y_ref, gate_ref, g_ref, sh_ref, sc_ref, w_ref, xo_ref, q_ref, k_ref, v_ref, *, qscale):
    x = x_ref[0] + gate_ref[0] * y_ref[...]
    xo_ref[0] = x
    h = _norm_mod(x, g_ref[...], sh_ref[0], sc_ref[0])
    r = _bdot(h, w_ref[...])
    d = x.shape[-1]
    npair = q_ref.shape[1]
    lanes = q_ref.shape[-1]
    for p in range(npair):
        q_ref[0, p] = (r[:, p * lanes:(p + 1) * lanes] * qscale).astype(BF16)
        k_ref[0, p] = r[:, d + p * lanes:d + (p + 1) * lanes].astype(BF16)
        v_ref[0, p] = r[:, 2 * d + p * lanes:2 * d + (p + 1) * lanes].astype(BF16)


def _flat_rows(row0, t, tm, d):
    return pl.BlockSpec((tm, d), lambda bi, i: (row0 // tm + bi * (t // tm) + i, 0))


def _qkv(x, y, row0, gate, g, shift, scale, w_qkv):
    b, t, d = x.shape
    tm = min(TOKEN_TILE, t)
    assert row0 % tm == 0
    npair = NA_HEADS // 2
    lanes = d // npair
    tok = pl.BlockSpec((1, tm, d), lambda bi, i: (bi, i, 0))
    row = pl.BlockSpec((1, 1, d), lambda bi, i: (bi, 0, 0))
    hp = pl.BlockSpec((1, npair, tm, lanes), lambda bi, i: (bi, 0, i, 0))
    hps = jax.ShapeDtypeStruct((b, npair, t, lanes), BF16)
    return pl.pallas_call(
        functools.partial(_qkv_kernel, qscale=float((d // NA_HEADS) ** -0.5)),
        grid=(b, t // tm),
        in_specs=[tok, _flat_rows(row0, t, tm, d), row, pl.BlockSpec((1, d), lambda bi, i: (0, 0)), row, row,
                  pl.BlockSpec(w_qkv.shape, lambda bi, i: (0, 0))],
        out_specs=[tok, hp, hp, hp],
        out_shape=[jax.ShapeDtypeStruct((b, t, d), F32), hps, hps, hps],
        compiler_params=_cparams("parallel", "parallel"),
        name="qkv",
    )(x, y, gate, g, shift, scale, w_qkv)


def _na_bias_tables(rpb):
    h = rpb.shape[0]
    col = jnp.arange(GRID_W)
    c_start = jnp.clip(col - NA_KC // 2, 0, GRID_W - NA_KC)
    col_ok = (col[None, :] >= c_start[:, None]) & (col[None, :] < c_start[:, None] + NA_KC)
    dc_idx = jnp.clip(col[None, :] - col[:, None] + NA_KC - 1, 0, 2 * NA_KC - 2)
    rc = jnp.where(col_ok[None, None], rpb[:, :, dc_idx].astype(F32), NEG_INF)
    dv = jnp.arange(NA_KR)[:, None]
    kr = jnp.arange(NA_KR)[None, :]
    t = rc[:, kr - dv + NA_KR - 1]
    t = t.reshape(h // 2, 2, NA_KR, NA_KR, GRID_W, GRID_W)
    return jnp.transpose(t, (0, 2, 1, 4, 3, 5)).reshape(h // 2, NA_KR, 2 * GRID_W, NA_KR * GRID_W)


def _attn_kernel(q_ref, kp_ref, kc_ref, kn_ref, vp_ref, vc_ref, vn_ref, kx_ref, vx_ref, bias_ref, o_ref,
                 kbuf, vbuf, q2_ref, sc_ref, pl_ref, pc_ref, li_ref, *, rows):
    i = pl.program_id(2)
    blk = kc_ref.shape[2]
    kbuf[0:blk, :] = kp_ref[0, 0]
    kbuf[blk:2 * blk, :] = kc_ref[0, 0]
    kbuf[2 * blk:3 * blk, :] = kn_ref[0, 0]
    vbuf[0:blk, :] = vp_ref[0, 0]
    vbuf[blk:2 * blk, :] = vc_ref[0, 0]
    vbuf[2 * blk:3 * blk, :] = vn_ref[0, 0]
    lanes = q_ref.shape[-1]
    w2 = 2 * GRID_W
    lane = lax.broadcasted_iota(jnp.int32, (GRID_W, lanes), 1)
    first = lane < lanes // 2
    nt_dims = (((1,), (1,)), ((), ()))
    for rho in range(NA_KR):
        q = q_ref[0, 0, rho * GRID_W:(rho + 1) * GRID_W, :]
        q2_ref[rho * w2:rho * w2 + GRID_W, :] = jnp.where(first, q, jnp.zeros_like(q))
        q2_ref[rho * w2 + GRID_W:(rho + 1) * w2, :] = jnp.where(first, jnp.zeros_like(q), q)
    sc_ref[...] = lax.dot_general(q2_ref[...], kx_ref[0, 0], nt_dims, preferred_element_type=F32)

    def window(rho):
        r = NA_KR * i + rho
        rs = jnp.clip(r - NA_KR // 2, 0, rows - NA_KR)
        return pl.multiple_of((rs - NA_KR * i + NA_KR) * GRID_W, GRID_W), r - rs

    for rho in range(NA_KR):
        wstart, dvar = window(rho)
        sl = slice(rho * w2, (rho + 1) * w2)
        s_loc = lax.dot_general(q2_ref[sl, :], kbuf[pl.ds(wstart, blk), :], nt_dims,
                                preferred_element_type=F32) + bias_ref[0, dvar]
        s_ctx = sc_ref[sl, :]
        m = jnp.maximum(jnp.max(s_loc, axis=-1, keepdims=True), jnp.max(s_ctx, axis=-1, keepdims=True))
        p_loc = jnp.exp(s_loc - m)
        p_ctx = jnp.exp(s_ctx - m)
        den = jnp.sum(p_loc, axis=-1, keepdims=True) + jnp.sum(p_ctx, axis=-1, keepdims=True)
        pl_ref[rho] = p_loc.astype(BF16)
        pc_ref[sl, :] = p_ctx.astype(BF16)
        li_ref[sl, :] = jnp.broadcast_to(1.0 / den, (w2, lanes))

    o_ctx = jnp.dot(pc_ref[...], vx_ref[0, 0], preferred_element_type=F32)
    for rho in range(NA_KR):
        wstart, _ = window(rho)
        sl = slice(rho * w2, (rho + 1) * w2)
        o = jnp.dot(pl_ref[rho], vbuf[pl.ds(wstart, blk), :], preferred_element_type=F32)
        o = (o + o_ctx[sl, :]) * li_ref[sl, :]
        o_ref[0, rho * GRID_W:(rho + 1) * GRID_W, :] = jnp.where(first, o[:GRID_W], o[GRID_W:]).astype(o_ref.dtype)


def _attention(q, k, v, kx, vx, bias):
    b, npair, t, lanes = q.shape
    c = kx.shape[2]
    blk = NA_KR * GRID_W
    nb = t // blk
    rows = t // GRID_W
    cur = lambda p, bi, i: (bi, p, i, 0)
    prv = lambda p, bi, i: (bi, p, jnp.maximum(i - 1, 0), 0)
    nxt = lambda p, bi, i: (bi, p, jnp.minimum(i + 1, nb - 1), 0)
    tb = lambda m: pl.BlockSpec((1, 1, blk, lanes), m)
    cx = pl.BlockSpec((1, 1, c, lanes), lambda p, bi, i: (bi, p, 0, 0))
    return pl.pallas_call(
        functools.partial(_attn_kernel, rows=rows),
        grid=(npair, b, nb),
        in_specs=[tb(cur), tb(prv), tb(cur), tb(nxt), tb(prv), tb(cur), tb(nxt), cx, cx,
                  pl.BlockSpec((1,) + bias.shape[1:], lambda p, bi, i: (p, 0, 0, 0))],
        out_specs=pl.BlockSpec((1, blk, lanes), lambda p, bi, i: (bi, i, p)),
        out_shape=jax.ShapeDtypeStruct((b, t, npair * lanes), BF16),
        scratch_shapes=[pltpu.VMEM((3 * blk, lanes), BF16), pltpu.VMEM((3 * blk, lanes), BF16),
                        pltpu.VMEM((2 * blk, lanes), BF16), pltpu.VMEM((2 * blk, c), F32),
                        pltpu.VMEM((NA_KR, 2 * GRID_W, blk), BF16), pltpu.VMEM((2 * blk, c), BF16),
                        pltpu.VMEM((2 * blk, lanes), F32)],
        compiler_params=_cparams("parallel", "parallel", "parallel"),
        name="na_attention",
    )(q, k, k, k, v, v, v, kx, vx, bias)


def _oproj_kernel(x_ref, o_ref, wo_ref, gate_ref, g2_ref, sh_ref, sc_ref, wr_ref, br_ref, xo_ref, h_ref, rec_ref):
    subs = _sub_rows(x_ref.shape[1])
    y = [jnp.dot(o_ref[0, r, :], wo_ref[...], preferred_element_type=F32) for r in subs]
    h = []
    for r, v in zip(subs, y):
        x = x_ref[0, r, :] + gate_ref[0] * v
        xo_ref[0, r, :] = x
        h.append(_norm_mod(x, g2_ref[...], sh_ref[0], sc_ref[0]))
    logits = [_route_logits(v, wr_ref, br_ref) for v in h]
    for r, v, lg in zip(subs, h, logits):
        _route_select(v, lg, h_ref, rec_ref, r)


def _oproj(x, o, w_out, gate, g2, shift, scale, wr, br):
    b, t, d = x.shape
    tm = min(TOKEN_TILE, t)
    tok = lambda n: pl.BlockSpec((1, tm, n), lambda bi, i: (bi, i, 0))
    row = pl.BlockSpec((1, 1, d), lambda bi, i: (bi, 0, 0))
    full = lambda shape: pl.BlockSpec(shape, lambda bi, i: (0,) * len(shape))
    return pl.pallas_call(
        _oproj_kernel,
        grid=(b, t // tm),
        in_specs=[tok(d), tok(d), full((d, d)), row, full((1, d)), row, row, full(wr.shape), full(br.shape)],
        out_specs=[tok(d), tok(d + ROUTE_LANES), pl.BlockSpec((1, REC_FIELDS, tm), lambda bi, i: (bi, 0, i))],
        out_shape=[jax.ShapeDtypeStruct((b, t, d), F32), jax.ShapeDtypeStruct((b, t, d + ROUTE_LANES), F32),
                   jax.ShapeDtypeStruct((b, REC_FIELDS, t), F32)],
        compiler_params=_cparams("parallel", "parallel"),
        name="oproj_route",
    )(x, o, w_out, gate, g2, shift, scale, wr, br)


def _final_kernel(x_ref, y_ref, gate_ref, g_ref, o_ref):
    x = x_ref[0] + gate_ref[0] * y_ref[...]
    ms = jnp.mean(x * x, axis=-1, keepdims=True)
    o_ref[0] = (x * lax.rsqrt(ms + RMS_EPS)) * g_ref[...]


def _final(x, y, gate, g):
    b, t, d = x.shape
    tm = min(WIDE_TILE, t)
    tok = pl.BlockSpec((1, tm, d), lambda bi, i: (bi, i, 0))
    return pl.pallas_call(
        _final_kernel,
        grid=(b, t // tm),
        in_specs=[tok, _flat_rows(0, t, tm, d), pl.BlockSpec((1, 1, d), lambda bi, i: (bi, 0, 0)),
                  pl.BlockSpec((1, d), lambda bi, i: (0, 0))],
        out_specs=tok,
        out_shape=jax.ShapeDtypeStruct((b, t, d), F32),
        compiler_params=_cparams("parallel", "parallel"),
        name="final_norm",
    )(x, y, gate, g)


def kernel(x, c, ctx, c_ctx, ada_w, ada_b, norm1_g, norm2_g, rec_w_in, rec_conv_w, rec_conv_b, lru_wa, lru_ba, lru_wx, lru_bx, lru_lambda, s5_a_re, s5_a_im, s5_log_dt, s5_b_re, s5_b_im, s5_c_re, s5_c_im, s5_d, s5_glu_w, s5_glu_b, rec_w_out, na_w_qkv, na_w_out, na_rpb, moe_r1_w, moe_r1_b, moe_r2_w, moe_r2_b, moe_w_gate, moe_w_up, moe_w_down, final_norm_g):
    b, t, d = x.shape
    tc = ctx.shape[1]
    assert ada_w.shape[0] == 2, "layer 0 recurrent mixer, layer 1 neighbourhood attention"
    w = rec_w_in.shape[-1] // 3

    rpad = -(b + 1) % 8
    cc = jnp.concatenate([c, c_ctx[None], jnp.zeros((rpad, d), F32)], axis=0)
    mod = _ada_mod(cc, ada_w, ada_b)

    def mods(layer, ctx_rows):
        rows = jnp.broadcast_to(mod[layer, b:b + 1], (b, 6 * d)) if ctx_rows else mod[layer, :b]
        return [rows[:, j * d:(j + 1) * d].reshape(b, 1, d) for j in range(6)]

    row = lambda v: v.reshape(1, -1)

    w_in = rec_w_in[0].astype(BF16)
    lru = [(_block_diag(lru_wa[0, dr]).astype(BF16), row(lru_ba[0, dr]), _block_diag(lru_wx[0, dr]).astype(BF16),
            row(lru_bx[0, dr]), row(lru_lambda[0, dr])) for dr in (0, 1)]
    s5t = _s5_tables(s5_a_re[0], s5_a_im[0], s5_log_dt[0], s5_b_re[0], s5_b_im[0], s5_c_re[0], s5_c_im[0])
    glu_w = s5_glu_w[0].astype(BF16)
    w_out0 = rec_w_out[0].astype(BF16)
    wr0, br0 = _router_tables(moe_r1_w[0], moe_r1_b[0], moe_r2_w[0], moe_r2_b[0])

    def mixer0(xs, m, h0_lru, h0_s5):
        xa, ga, ub = _inproj(xs, row(norm1_g[0]), m[0], m[1], w_in)
        hf = _lru_dir(xa, rec_conv_w[0], row(rec_conv_b[0]), *lru[0], h0_lru[0], False)
        hb = _lru_dir(xa, rec_conv_w[0], row(rec_conv_b[0]), *lru[1], h0_lru[1], True)
        ys, s5_fin = _s5(ub, s5t, h0_s5)
        x_mid, h2, rec = _merge(xs, ga, hf, hb, ub, ys, row(s5_d[0]), glu_w, row(s5_glu_b[0]), w_out0, m[2],
                                row(norm2_g[0]), m[3], m[4], wr0, br0)
        return x_mid, h2, rec, (hf[:, -1:], hb[:, :1]), s5_fin

    zl = jnp.zeros((b, 1, w), F32)
    zs = jnp.zeros((s5t[0].shape[0], b, s5t[0].shape[-1]), F32)
    mc0, ml0 = mods(0, True), mods(0, False)
    xc_mid, hc2, recc, lru_fin, s5_fin = mixer0(ctx, mc0, (zl, zl), zs)
    xl_mid, hl2, recl, _, _ = mixer0(x, ml0, lru_fin, s5_fin)

    h_all = jnp.concatenate([hl2.reshape(b * t, -1), hc2.reshape(b * tc, -1)], axis=0)
    field = lambda k: jnp.concatenate([recl[:, k].reshape(b * t), recc[:, k].reshape(b * tc)])
    y_all = _experts(h_all, field(2), field(3), moe_w_gate[0].astype(BF16), moe_w_up[0].astype(BF16),
                     moe_w_down[0].astype(BF16))

    w_qkv = na_w_qkv[0].astype(BF16)
    mc1, ml1 = mods(1, True), mods(1, False)
    _, _, kx, vx = _qkv(xc_mid, y_all, b * t, mc0[5], row(norm1_g[1]), mc1[0], mc1[1], w_qkv)
    x1, q, k, v = _qkv(xl_mid, y_all, 0, ml0[5], row(norm1_g[1]), ml1[0], ml1[1], w_qkv)
    o = _attention(q, k, v, kx, vx, _na_bias_tables(na_rpb[0]))
    wr1, br1 = _router_tables(moe_r1_w[1], moe_r1_b[1], moe_r2_w[1], moe_r2_b[1])
    x1_mid, h2, rec = _oproj(x1, o, na_w_out[0].astype(BF16), ml1[2], row(norm2_g[1]), ml1[3], ml1[4], wr1, br1)
    y1 = _experts(h2.reshape(b * t, -1), rec[:, 2].reshape(b * t), rec[:, 3].reshape(b * t), moe_w_gate[1].astype(BF16),
                  moe_w_up[1].astype(BF16), moe_w_down[1].astype(BF16))
    return _final(x1_mid, y1, ml1[5], row(final_norm_g))
```

```python
import functools

import jax
import jax.numpy as jnp
from jax import lax
from jax.experimental import pallas as pl
from jax.experimental.pallas import tpu as pltpu

F32 = jnp.float32
BF16 = jnp.bfloat16
HIGHEST = lax.Precision.HIGHEST

RMS_EPS = 1e-6
GRID_W = 64
LRU_C = 8.0
S5_GROUP = 16
S5_CHUNK = 16
NA_HEADS = 16
NA_KR = 8
NA_KC = 16
NEG_INF = -1e30
MOE_GROUPS = 4
MOE_PER_GROUP = 8
MOE_EXPERTS = MOE_GROUPS * MOE_PER_GROUP
MOE_PAIRS = MOE_GROUPS * (MOE_PER_GROUP * (MOE_PER_GROUP - 1) // 2)
EXPERT_TILE = 256
ROUTE_LANES = 128
REC_FIELDS = 8
TOKEN_TILE = 512
WIDE_TILE = 1024
VMEM_LIMIT = 56 * 1024 * 1024


def _cparams(*sem):
    return pltpu.CompilerParams(dimension_semantics=sem, vmem_limit_bytes=VMEM_LIMIT)


def _norm_mod(x, g, shift, scale):
    ms = jnp.mean(x * x, axis=-1, keepdims=True)
    return (x * lax.rsqrt(ms + RMS_EPS)) * g * (1.0 + scale) + shift


def _silu(x):
    return x * jax.nn.sigmoid(x)


def _gelu(x):
    return jax.nn.gelu(x, approximate=True)


def _bdot(a, b):
    return jnp.dot(a.astype(BF16), b, preferred_element_type=F32)


def _ada_kernel(c_ref, w_ref, b_ref, o_ref):
    o_ref[0] = jnp.dot(_silu(c_ref[...]), w_ref[0], preferred_element_type=F32,
                       precision=HIGHEST) + b_ref[0]


def _ada_mod(cc, ada_w, ada_b):
    n_layers, d, d6 = ada_w.shape
    r = cc.shape[0]
    return pl.pallas_call(
        _ada_kernel,
        grid=(n_layers, d6 // d),
        in_specs=[pl.BlockSpec((r, d), lambda l, j: (0, 0)),
                  pl.BlockSpec((1, d, d), lambda l, j: (l, 0, j)),
                  pl.BlockSpec((1, 1, d), lambda l, j: (l, 0, j))],
        out_specs=pl.BlockSpec((1, r, d), lambda l, j: (l, 0, j)),
        out_shape=jax.ShapeDtypeStruct((n_layers, r, d6), F32),
        compiler_params=_cparams("arbitrary", "arbitrary"),
        name="ada_mod",
    )(cc, ada_w, ada_b.reshape(n_layers, 1, d6))


def _inproj_kernel(x_ref, g_ref, sh_ref, sc_ref, w_ref, xa_ref, ga_ref, ub_ref):
    h = _norm_mod(x_ref[0], g_ref[...], sh_ref[0], sc_ref[0])
    r = _bdot(h, w_ref[...])
    w = xa_ref.shape[-1]
    xa_ref[0] = r[:, :w]
    ga_ref[0] = r[:, w:2 * w]
    ub_ref[0] = r[:, 2 * w:]


def _inproj(x, g, shift, scale, w_in):
    b, t, d = x.shape
    w = w_in.shape[1] // 3
    tm = min(WIDE_TILE, t)
    row = pl.BlockSpec((1, 1, d), lambda bi, i: (bi, 0, 0))
    out = pl.BlockSpec((1, tm, w), lambda bi, i: (bi, i, 0))
    return pl.pallas_call(
        _inproj_kernel,
        grid=(b, t // tm),
        in_specs=[pl.BlockSpec((1, tm, d), lambda bi, i: (bi, i, 0)),
                  pl.BlockSpec((1, d), lambda bi, i: (0, 0)), row, row,
                  pl.BlockSpec(w_in.shape, lambda bi, i: (0, 0))],
        out_specs=[out, out, out],
        out_shape=[jax.ShapeDtypeStruct((b, t, w), F32)] * 3,
        compiler_params=_cparams("parallel", "parallel"),
        name="inproj",
    )(x, g, shift, scale, w_in)


def _lru_kernel(xc_ref, xp_ref, xn_ref, cw_ref, cb_ref, wa_ref, ba_ref, wx_ref, bx_ref, lam_ref,
                h0_ref, o_ref, ext_ref, a_ref, b_ref, car_ref, *, reverse, nt, tt):
    i = pl.program_id(1)
    ti = (nt - 1 - i) if reverse else i
    w = o_ref.shape[-1]

    @pl.when(i == 0)
    def _():
        car_ref[...] = h0_ref[0]

    ext_ref[0:8, :] = jnp.where(ti == 0, 0.0, xp_ref[0])
    ext_ref[8:8 + tt, :] = xc_ref[0]
    ext_ref[8 + tt:16 + tt, :] = jnp.where(ti == nt - 1, 0.0, xn_ref[0])
    cw = cw_ref[...]
    u = (ext_ref[6:6 + tt, :] * cw[0:1] + ext_ref[7:7 + tt, :] * cw[1:2]
         + ext_ref[8:8 + tt, :] * cw[2:3] + ext_ref[9:9 + tt, :] * cw[3:4]) + cb_ref[...]
    r = jax.nn.sigmoid(_bdot(u, wa_ref[...]) + ba_ref[...])
    ig = jax.nn.sigmoid(_bdot(u, wx_ref[...]) + bx_ref[...])
    log_a = (-LRU_C) * r * jax.nn.softplus(-lam_ref[...])
    a = jnp.exp(log_a)
    a_ref[...] = a
    b_ref[...] = jnp.sqrt(-jnp.tanh(log_a) * (1.0 + a * a)) * (ig * u)

    nsl = tt // 8
    row = lax.broadcasted_iota(jnp.int32, (8, w), 0)

    def slab(s, carry):
        off = pl.multiple_of(((nsl - 1 - s) if reverse else s) * 8, 8)
        a = a_ref[pl.ds(off, 8), :]
        bb = b_ref[pl.ds(off, 8), :]
        for k in (1, 2, 4):
            valid = (row < 8 - k) if reverse else (row >= k)
            sh = (8 - k) if reverse else k
            a_s = jnp.where(valid, pltpu.roll(a, sh, 0), 1.0)
            b_s = jnp.where(valid, pltpu.roll(bb, sh, 0), 0.0)
            bb = bb + a * b_s
            a = a * a_s
        h = bb + a * carry
        o_ref[0, pl.ds(off, 8), :] = h
        return h[0:1] if reverse else h[7:8]

    car_ref[...] = lax.fori_loop(0, nsl, slab, car_ref[...])


def _lru_dir(xa, conv_w, conv_b, wa_bd, ba, wx_bd, bx, lam, h0, reverse):
    b, t, w = xa.shape
    tt = min(WIDE_TILE, t)
    nt = t // tt
    hb = tt // 8
    tile = (lambda i: nt - 1 - i) if reverse else (lambda i: i)
    full = lambda shape: pl.BlockSpec(shape, lambda bi, i: (0,) * len(shape))
    return pl.pallas_call(
        functools.partial(_lru_kernel, reverse=reverse, nt=nt, tt=tt),
        grid=(b, nt),
        in_specs=[pl.BlockSpec((1, tt, w), lambda bi, i: (bi, tile(i), 0)),
                  pl.BlockSpec((1, 8, w), lambda bi, i: (bi, jnp.maximum(tile(i) * hb - 1, 0), 0)),
                  pl.BlockSpec((1, 8, w), lambda bi, i: (bi, jnp.minimum((tile(i) + 1) * hb, t // 8 - 1), 0)),
                  full(conv_w.shape), full((1, w)), full((w, w)), full((1, w)), full((w, w)),
                  full((1, w)), full((1, w)),
                  pl.BlockSpec((1, 1, w), lambda bi, i: (bi, 0, 0))],
        out_specs=pl.BlockSpec((1, tt, w), lambda bi, i: (bi, tile(i), 0)),
        out_shape=jax.ShapeDtypeStruct((b, t, w), F32),
        scratch_shapes=[pltpu.VMEM((tt + 16, w), F32), pltpu.VMEM((tt, w), F32),
                        pltpu.VMEM((tt, w), F32), pltpu.VMEM((1, w), F32)],
        compiler_params=_cparams("parallel", "arbitrary"),
        name="lru_bwd" if reverse else "lru_fwd",
    )(xa, xa, xa, conv_w, conv_b, wa_bd, ba, wx_bd, bx, lam, h0)


def _block_diag(w):
    h, d, _ = w.shape
    eye = jnp.eye(h, dtype=w.dtype)
    return (eye[:, None, :, None] * w[:, :, None, :]).reshape(h * d, h * d)


def _s5_tables(a_re, a_im, log_dt, b_re, b_im, c_re, c_im):
    L = S5_CHUNK
    g, n = a_re.shape[1], a_re.shape[2]
    p = b_re.shape[-1]
    f = lambda x: x.astype(F32)
    a_re, a_im, b_re, b_im, c_re, c_im = map(f, (a_re, a_im, b_re, b_im, c_re, c_im))
    dt = jnp.exp(f(log_dt))[..., None]
    mag = jnp.exp(a_re * dt)
    lb_re, lb_im = mag * jnp.cos(a_im * dt), mag * jnp.sin(a_im * dt)
    den = a_re * a_re + a_im * a_im
    q_re = ((lb_re - 1.0) * a_re + lb_im * a_im) / den
    q_im = (lb_im * a_re - (lb_re - 1.0) * a_im) / den
    bb_re = q_re[..., None] * b_re - q_im[..., None] * b_im
    bb_im = q_re[..., None] * b_im + q_im[..., None] * b_re
    pw_re, pw_im = [jnp.ones_like(lb_re)], [jnp.zeros_like(lb_im)]
    for _ in range(L):
        r_, i_ = pw_re[-1], pw_im[-1]
        pw_re.append(r_ * lb_re - i_ * lb_im)
        pw_im.append(r_ * lb_im + i_ * lb_re)
    pw_re, pw_im = jnp.stack(pw_re, 1), jnp.stack(pw_im, 1)
    es = functools.partial(jnp.einsum, precision=HIGHEST)
    kf = jnp.arange(L - 1, -1, -1)
    kb = jnp.arange(L)
    inc = []
    for d_, ks in ((0, kf), (1, kb)):
        pr, pi = pw_re[d_][ks], pw_im[d_][ks]
        inc.append((pr[..., None] * bb_re[d_] - pi[..., None] * bb_im[d_],
                    pr[..., None] * bb_im[d_] + pi[..., None] * bb_re[d_]))
    gq = 128 // p
    nq = g // gq

    def quad_block_diag(a, outer, rpg, cpg):
        cols = a.shape[-1]
        full = jnp.broadcast_to(a[:, :, None], (nq, outer, gq, rpg, cols)).reshape(nq, outer * gq * rpg, cols)
        rg = (lax.broadcasted_iota(jnp.int32, full.shape, 1) // rpg) % gq
        cg = (lax.broadcasted_iota(jnp.int32, full.shape, 2) // cpg) % gq
        return jnp.where(rg == cg, full, 0.0).astype(BF16)

    inc_all = jnp.stack([jnp.stack(inc[0]), jnp.stack(inc[1])]).reshape(2, 2, L, nq, gq, n, p)
    wb = jnp.transpose(inc_all, (3, 2, 6, 0, 1, 4, 5)).reshape(nq, L, p, 4 * gq * n)
    wb = quad_block_diag(wb, L, p, n)
    rd = []
    for d_, ks in ((0, jnp.arange(1, L + 1)), (1, jnp.arange(L, 0, -1))):
        pr, pi = pw_re[d_][ks], pw_im[d_][ks]
        cl_re = c_re[d_][None] * pr[:, :, None, :] - c_im[d_][None] * pi[:, :, None, :]
        cl_im = c_re[d_][None] * pi[:, :, None, :] + c_im[d_][None] * pr[:, :, None, :]
        rd.append((cl_re, -cl_im))
    rd_all = jnp.stack([jnp.stack(rd[0]), jnp.stack(rd[1])]).reshape(2, 2, L, nq, gq, p, n)
    wc = jnp.transpose(rd_all, (3, 0, 1, 6, 2, 4, 5)).reshape(nq, 4, n, L * gq * p)
    wc = quad_block_diag(wc, 4, n, p)
    ker = []
    for d_ in (0, 1):
        pr, pi = pw_re[d_][:L], pw_im[d_][:L]
        cl_re = c_re[d_][None] * pr[:, :, None, :] - c_im[d_][None] * pi[:, :, None, :]
        cl_im = c_re[d_][None] * pi[:, :, None, :] + c_im[d_][None] * pr[:, :, None, :]
        ker.append(es('kgpn,gnq->kgpq', cl_re, bb_re[d_]) - es('kgpn,gnq->kgpq', cl_im, bb_im[d_]))
    s_i = jnp.arange(L)[:, None]
    t_i = jnp.arange(L)[None, :]
    kf_t = ker[0][jnp.clip(t_i - s_i, 0, L - 1)]
    kb_t = ker[1][jnp.clip(s_i - t_i, 0, L - 1)]
    m = (jnp.where((s_i <= t_i)[:, :, None, None, None], kf_t, 0.0)
         + jnp.where((s_i >= t_i)[:, :, None, None, None], kb_t, 0.0))
    mt = jnp.transpose(m.reshape(L, L, nq, gq, p, p), (2, 0, 5, 1, 3, 4)).reshape(nq, L, p, L * gq * p)
    mt = quad_block_diag(mt, L, p, p)
    l16 = jnp.stack([jnp.stack([pw_re[d_][L], pw_im[d_][L]]) for d_ in (0, 1)])
    l16 = l16.reshape(2, 2, nq, gq * n)
    return wb, mt, wc, l16


def _s5_inc_kernel(x_ref, wb_ref, xs_ref, s_ref):
    lanes = x_ref.shape[-1]
    nj = xs_ref.shape[2]
    for tau in range(S5_CHUNK):
        xs_ref[0, 0, :, tau * lanes:(tau + 1) * lanes] = x_ref[0, pl.ds(tau, nj, stride=S5_CHUNK), :].astype(BF16)
    s_ref[0, 0] = jnp.dot(xs_ref[0, 0], wb_ref[0], preferred_element_type=F32)


def _s5_scan_kernel(sf_ref, sb_ref, l16_ref, h0_ref, hf_ref, hb_ref, hfin_ref, st_ref, *, jb):
    i = pl.program_id(0)
    nq = st_ref.shape[0]
    dl = sf_ref.shape[-1]
    w = dl // 2

    @pl.when(i == 0)
    def _():
        st_ref[...] = h0_ref[...]

    for jj in range(jb):
        for q in range(nq):
            for d, (src, dst, row) in enumerate(((sf_ref, hf_ref, jj), (sb_ref, hb_ref, jb - 1 - jj))):
                s = src[q, :, row, :]
                h = st_ref[q, :, d * dl:(d + 1) * dl]
                dst[q, :, row, :] = h
                lr, li = l16_ref[d, 0, q:q + 1, :], l16_ref[d, 1, q:q + 1, :]
                hr, hi = h[:, :w], h[:, w:]
                st_ref[q, :, d * dl:(d + 1) * dl] = jnp.concatenate(
                    [lr * hr - li * hi + s[:, :w], lr * hi + li * hr + s[:, w:]], axis=-1)

    @pl.when(i == pl.num_programs(0) - 1)
    def _():
        hfin_ref[...] = st_ref[...]


def _s5_out_kernel(xs_ref, hf_ref, hb_ref, mt_ref, wc_ref, y_ref):
    dl = hf_ref.shape[-1]
    y = (jnp.dot(xs_ref[0, 0], mt_ref[0], preferred_element_type=F32)
         + _bdot(hf_ref[0, 0], wc_ref[0, 0:dl, :]) + _bdot(hb_ref[0, 0], wc_ref[0, dl:2 * dl, :]))
    lanes = y_ref.shape[-1]
    for k in range(y_ref.shape[2]):
        y_ref[0, :, k, :] = y[:, k * lanes:(k + 1) * lanes]


def _s5(ub, tables, h0):
    wb, mt, wc, l16 = tables
    b, t, w = ub.shape
    L = S5_CHUNK
    nj = t // L
    nq = wb.shape[0]
    lanes = w // nq
    cl = L * lanes
    sl = wb.shape[-1]
    xs, s = pl.pallas_call(
        _s5_inc_kernel,
        grid=(nq, b),
        in_specs=[pl.BlockSpec((1, t, lanes), lambda q, bi: (bi, 0, q)),
                  pl.BlockSpec((1, cl, sl), lambda q, bi: (q, 0, 0))],
        out_specs=[pl.BlockSpec((1, 1, nj, cl), lambda q, bi: (q, bi, 0, 0)),
                   pl.BlockSpec((1, 1, nj, sl), lambda q, bi: (q, bi, 0, 0))],
        out_shape=[jax.ShapeDtypeStruct((nq, b, nj, cl), BF16), jax.ShapeDtypeStruct((nq, b, nj, sl), F32)],
        compiler_params=_cparams("parallel", "parallel"),
        name="s5_inc",
    )(ub, wb)
    jb = 8
    nblk = nj // jb
    dl = sl // 2
    hf, hb, hfin = pl.pallas_call(
        functools.partial(_s5_scan_kernel, jb=jb),
        grid=(nblk,),
        in_specs=[pl.BlockSpec((nq, b, jb, dl), lambda i: (0, 0, i, 0)),
                  pl.BlockSpec((nq, b, jb, dl), lambda i: (0, 0, nblk - 1 - i, 1)),
                  pl.BlockSpec(l16.shape, lambda i: (0, 0, 0, 0)),
                  pl.BlockSpec((nq, b, sl), lambda i: (0, 0, 0))],
        out_specs=[pl.BlockSpec((nq, b, jb, dl), lambda i: (0, 0, i, 0)),
                   pl.BlockSpec((nq, b, jb, dl), lambda i: (0, 0, nblk - 1 - i, 0)),
                   pl.BlockSpec((nq, b, sl), lambda i: (0, 0, 0))],
        out_shape=[jax.ShapeDtypeStruct((nq, b, nj, dl), F32), jax.ShapeDtypeStruct((nq, b, nj, dl), F32),
                   jax.ShapeDtypeStruct((nq, b, sl), F32)],
        scratch_shapes=[pltpu.VMEM((nq, b, sl), F32)],
        compiler_params=_cparams("arbitrary"),
        name="s5_scan",
    )(s, s, l16, h0)
    nh = 2
    y = pl.pallas_call(
        _s5_out_kernel,
        grid=(nq, nh, b),
        in_specs=[pl.BlockSpec((1, 1, nj, cl), lambda q, h, bi: (q, bi, 0, 0)),
                  pl.BlockSpec((1, 1, nj, dl), lambda q, h, bi: (q, bi, 0, 0)),
                  pl.BlockSpec((1, 1, nj, dl), lambda q, h, bi: (q, bi, 0, 0)),
                  pl.BlockSpec((1, cl, cl // nh), lambda q, h, bi: (q, 0, h)),
                  pl.BlockSpec((1, sl, cl // nh), lambda q, h, bi: (q, 0, h))],
        out_specs=pl.BlockSpec((1, nj, L // nh, lanes), lambda q, h, bi: (bi, 0, h, q)),
        out_shape=jax.ShapeDtypeStruct((b, nj, L, w), F32),
        compiler_params=_cparams("parallel", "parallel", "parallel"),
        name="s5_out",
    )(xs, hf, hb, mt, wc)
    return y.reshape(b, t, w), hfin


SUB_ROWS = 256


def _sub_rows(tm):
    return [slice(i, i + min(SUB_ROWS, tm)) for i in range(0, tm, min(SUB_ROWS, tm))]


def _route_logits(h, wr_ref, br_ref):
    h_hi = h.astype(BF16)
    h_lo = (h - h_hi.astype(F32)).astype(BF16)
    t = jnp.dot(h_hi, wr_ref[...], preferred_element_type=F32)
    return (t[:, :ROUTE_LANES] + t[:, ROUTE_LANES:]
            + jnp.dot(h_lo, wr_ref[:, 0:ROUTE_LANES], preferred_element_type=F32) + br_ref[...])


def _route_select(h, logits, hx_ref, rec_ref, rows):
    lane = lax.broadcasted_iota(jnp.int32, logits.shape, 1)
    big = jnp.int32(ROUTE_LANES)
    l1 = jnp.where(lane < MOE_GROUPS, logits, NEG_INF)
    m1 = jnp.max(l1, axis=-1, keepdims=True)
    gidx = jnp.min(jnp.where(l1 == m1, lane, big), axis=-1, keepdims=True)
    gval = 1.0 / jnp.sum(jnp.where(lane < MOE_GROUPS, jnp.exp(logits - m1), 0.0), axis=-1, keepdims=True)
    lo = MOE_GROUPS + MOE_PER_GROUP * gidx
    l2 = jnp.where((lane >= lo) & (lane < lo + MOE_PER_GROUP), logits, NEG_INF)
    v1 = jnp.max(l2, axis=-1, keepdims=True)
    i1 = jnp.min(jnp.where(l2 == v1, lane, big), axis=-1, keepdims=True)
    l2 = jnp.where(lane == i1, NEG_INF, l2)
    v2 = jnp.max(l2, axis=-1, keepdims=True)
    i2 = jnp.min(jnp.where(l2 == v2, lane, big), axis=-1, keepdims=True)
    e = jnp.exp(v2 - v1)
    wa = gval / (1.0 + e)
    wb = wa * e
    first = i1 <= i2
    w_lo, w_hi = jnp.where(first, wa, wb), jnp.where(first, wb, wa)
    e_lo = (jnp.minimum(i1, i2) - MOE_GROUPS).astype(F32)
    e_hi = (jnp.maximum(i1, i2) - MOE_GROUPS).astype(F32)
    rec = jnp.where(lane == 0, w_lo, jnp.where(lane == 1, w_hi, jnp.where(lane == 2, e_lo,
                    jnp.where(lane == 3, e_hi, 0.0))))
    d = h.shape[-1]
    hx_ref[0, rows, 0:d] = h
    hx_ref[0, rows, d:d + ROUTE_LANES] = rec
    rec_ref[0, :, rows] = rec.T[0:REC_FIELDS, :]


def _router_tables(r1_w, r1_b, r2_w, r2_b):
    d = r1_w.shape[0]
    wr = jnp.zeros((d, ROUTE_LANES), F32)
    wr = wr.at[:, :MOE_GROUPS].set(r1_w)
    wr = wr.at[:, MOE_GROUPS:MOE_GROUPS + MOE_EXPERTS].set(jnp.transpose(r2_w, (1, 0, 2)).reshape(d, MOE_EXPERTS))
    br = jnp.zeros((1, ROUTE_LANES), F32)
    br = br.at[0, :MOE_GROUPS].set(r1_b)
    br = br.at[0, MOE_GROUPS:MOE_GROUPS + MOE_EXPERTS].set(r2_b.reshape(MOE_EXPERTS))
    wr_hi = wr.astype(BF16)
    wr_lo = (wr - wr_hi.astype(F32)).astype(BF16)
    return jnp.concatenate([wr_hi, wr_lo], axis=1), br


def _merge_kernel(x_ref, ga_ref, hf_ref, hb_ref, ub_ref, ys_ref, d_ref, gw_ref, gb_ref, wo_ref, gate_ref,
                  g2_ref, sh_ref, sc_ref, wr_ref, br_ref, xo_ref, h_ref, rec_ref):
    w = ga_ref.shape[-1]
    subs = _sub_rows(x_ref.shape[1])
    y_a = [((hf_ref[0, r, :] + hb_ref[0, r, :]) * _gelu(ga_ref[0, r, :])).astype(BF16) for r in subs]
    y_s = [_gelu(ys_ref[0, r, :] + d_ref[...] * ub_ref[0, r, :]) for r in subs]
    glu = [_bdot(v, gw_ref[...]) for v in y_s]
    y_s = [(v * jax.nn.sigmoid(g + gb_ref[...])).astype(BF16) for v, g in zip(y_s, glu)]
    y = [jnp.dot(a, wo_ref[0:w, :], preferred_element_type=F32)
         + jnp.dot(s, wo_ref[w:2 * w, :], preferred_element_type=F32) for a, s in zip(y_a, y_s)]
    h = []
    for r, v in zip(subs, y):
        x = x_ref[0, r, :] + gate_ref[0] * v
        xo_ref[0, r, :] = x
        h.append(_norm_mod(x, g2_ref[...], sh_ref[0], sc_ref[0]))
    logits = [_route_logits(v, wr_ref, br_ref) for v in h]
    for r, v, lg in zip(subs, h, logits):
        _route_select(v, lg, h_ref, rec_ref, r)


def _merge(x, ga, hf, hb, ub, ys, s5_d, glu_w, glu_b, w_out, gate, g2, shift, scale, wr, br):
    b, t, d = x.shape
    w = ga.shape[-1]
    tm = min(TOKEN_TILE, t)
    tok = lambda n: pl.BlockSpec((1, tm, n), lambda bi, i: (bi, i, 0))
    row = pl.BlockSpec((1, 1, d), lambda bi, i: (bi, 0, 0))
    full = lambda shape: pl.BlockSpec(shape, lambda bi, i: (0,) * len(shape))
    return pl.pallas_call(
        _merge_kernel,
        grid=(b, t // tm),
        in_specs=[tok(d), tok(w), tok(w), tok(w), tok(w), tok(w), full((1, w)), full((w, w)), full((1, w)),
                  full((2 * w, d)), row, full((1, d)), row, row, full(wr.shape), full(br.shape)],
        out_specs=[tok(d), tok(d + ROUTE_LANES), pl.BlockSpec((1, REC_FIELDS, tm), lambda bi, i: (bi, 0, i))],
        out_shape=[jax.ShapeDtypeStruct((b, t, d), F32), jax.ShapeDtypeStruct((b, t, d + ROUTE_LANES), F32),
                   jax.ShapeDtypeStruct((b, REC_FIELDS, t), F32)],
        compiler_params=_cparams("parallel", "parallel"),
        name="merge_route",
    )(x, ga, hf, hb, ub, ys, s5_d, glu_w, glu_b, w_out, gate, g2, shift, scale, wr, br)


DMA_GROUP = 8


EXPERT_CHUNK = 256


GATHER_AHEAD = 2


def _expert_kernel(elo_ref, ehi_ref, nrows_ref, g0_ref, g1_ref, g2_ref, sprv_ref, scur_ref, hx_hbm,
                   gl_ref, ul_ref, dl_ref, gh_ref, uh_ref, dh_ref, y_hbm, xb, yb, gsem, ssem):
    t = pl.program_id(0)
    nt = pl.num_programs(0)
    n = nrows_ref[t]
    _, tm, d = yb.shape
    f = gl_ref.shape[-1]
    nxb = GATHER_AHEAD + 1

    def gather_row(iref, r, s):
        return pltpu.make_async_copy(hx_hbm.at[pl.ds(iref[0, 0, r], 1)], xb.at[s, pl.ds(r, 1)], gsem.at[s])

    def scatter_row(iref, r, s):
        return pltpu.make_async_copy(yb.at[s, pl.ds(r, 1)], y_hbm.at[pl.ds(iref[0, 0, r], 1)], ssem.at[s])

    def gather_wait(s):
        pltpu.make_async_copy(hx_hbm.at[pl.ds(0, tm)], xb.at[s], gsem.at[s]).wait()

    def scatter_wait(s):
        pltpu.make_async_copy(yb.at[s], y_hbm.at[pl.ds(0, tm)], ssem.at[s]).wait()

    def all_rows(fn):
        g = DMA_GROUP
        lax.fori_loop(0, tm // g, lambda k, c: ([fn(k * g + j) for j in range(g)], c)[1], 0)

    @pl.when(t == 0)
    def _():
        yb[...] = jnp.zeros_like(yb)
        first_trash = pltpu.make_async_copy(yb.at[0], y_hbm.at[pl.ds(y_hbm.shape[0] - 2 * tm, tm)], ssem.at[0])
        first_trash.start()
        first_trash.wait()
        all_rows(lambda r: gather_row(g0_ref, r, 0).start())
        all_rows(lambda r: gather_row(g1_ref, r, 1).start())

    s = t % 2
    o = 1 - s
    cur = t % nxb
    nx1 = (t + 1) % nxb
    nx2 = (t + 2) % nxb

    def step():
        gather_wait(cur)
        x = xb[cur, :, 0:d].astype(BF16)
        gates = (xb[cur, :, d:d + 1], xb[cur, :, d + 1:d + 2])
        nchunk = f // EXPERT_CHUNK
        per = tm // (2 * nchunk)
        hids = []
        for e, (g_ref, u_ref) in enumerate(((gl_ref, ul_ref), (gh_ref, uh_ref))):
            for k in range(nchunk):
                c = e * nchunk + k
                for r in range(c * per, (c + 1) * per):
                    gather_row(g2_ref, r, nx2).start()
                    scatter_row(sprv_ref, r, o).start()
                cs = slice(k * EXPERT_CHUNK, (k + 1) * EXPERT_CHUNK)
                hid = _silu(jnp.dot(x, g_ref[0, :, cs], preferred_element_type=F32)) * jnp.dot(
                    x, u_ref[0, :, cs], preferred_element_type=F32)
                hids.append((hid * gates[e]).astype(BF16))
        y = None
        for e, d_ref in enumerate((dl_ref, dh_ref)):
            for k in range(nchunk):
                part = jnp.dot(hids[e * nchunk + k], d_ref[0, k * EXPERT_CHUNK:(k + 1) * EXPERT_CHUNK, :],
                               preferred_element_type=F32)
                y = part if y is None else y + part

        @pl.when(t > 0)
        def _():
            scatter_wait(s)
        yb[s] = y

        @pl.when(t == nt - 1)
        def _():
            gather_wait(nx1)
            gather_wait(nx2)
            scatter_wait(o)
            all_rows(lambda r: scatter_row(scur_ref, r, s).start())
            scatter_wait(s)

    def drain():
        gather_wait(cur)
        gather_wait(nx1)
        scatter_wait(s)
        all_rows(lambda r: scatter_row(sprv_ref, r, o).start())
        scatter_wait(o)

    had_rows = nrows_ref[jnp.maximum(t - 1, 0)] > 0
    pl.when(n > 0)(step)
    pl.when((n == 0) & (t > 0) & had_rows)(drain)


def _moe_schedule(e_lo, e_hi, tm):
    n = e_lo.shape[0]
    lo = jnp.clip(e_lo.astype(jnp.int32), 0, MOE_EXPERTS - 1)
    hi = jnp.clip(e_hi.astype(jnp.int32), 0, MOE_EXPERTS - 1)
    nbk = MOE_EXPERTS * MOE_EXPERTS
    bucket = lo * MOE_EXPERTS + hi
    order = jnp.argsort(bucket, stable=True).astype(jnp.int32)
    eids = jnp.arange(MOE_EXPERTS, dtype=jnp.int32)[None, :]
    count = jnp.einsum('nl,nh->lh', (lo[:, None] == eids).astype(F32), (hi[:, None] == eids).astype(F32),
                       precision=HIGHEST).astype(jnp.int32).reshape(nbk)
    start = jnp.cumsum(count) - count
    tiles = (count + tm - 1) // tm
    tile_end = jnp.cumsum(tiles)
    nt = n // tm + MOE_PAIRS
    tix = jnp.arange(nt, dtype=jnp.int32)
    total = tile_end[-1]
    bk = jnp.sum((tile_end[None, :] <= jnp.minimum(tix, total - 1)[:, None]).astype(jnp.int32), axis=1)
    bk = jnp.clip(bk, 0, nbk - 1)
    k = tix - (tile_end[bk] - tiles[bk])
    nrows = jnp.where(tix < total, jnp.clip(count[bk] - k * tm, 0, tm), 0).astype(jnp.int32)
    r = jnp.arange(tm, dtype=jnp.int32)[None, :]
    pos = jnp.clip(start[bk][:, None] + k[:, None] * tm + r, 0, n - 1)
    gidx = order[pos]
    sidx = jnp.where(r < nrows[:, None], gidx, n + (tix % 2)[:, None] * tm + r)
    sprev = jnp.concatenate([n + tm + r, sidx[:-1]], axis=0)
    shape = (nt, 1, tm)
    return (bk // MOE_EXPERTS, bk % MOE_EXPERTS, nrows, gidx.reshape(shape), sprev.reshape(shape),
            sidx.reshape(shape))


def _experts(hx, e_lo, e_hi, w_gate, w_up, w_down):
    n, dx = hx.shape
    d = dx - ROUTE_LANES
    f = w_gate.shape[-1]
    tm = EXPERT_TILE
    elo, ehi, nrows, gidx, sprev, sidx = _moe_schedule(e_lo, e_hi, tm)
    nt = nrows.shape[0]
    wspec = lambda shape, which: pl.BlockSpec(
        (1,) + shape, (lambda t, elo, ehi, nr: (elo[t], 0, 0)) if which == 0 else (lambda t, elo, ehi, nr: (ehi[t], 0, 0)))
    ispec = lambda m: pl.BlockSpec((1, 1, tm), m, memory_space=pltpu.SMEM)
    gs = pltpu.PrefetchScalarGridSpec(
        num_scalar_prefetch=3,
        grid=(nt,),
        in_specs=[ispec(lambda t, *_: (t, 0, 0)), ispec(lambda t, *_: (jnp.minimum(t + 1, nt - 1), 0, 0)),
                  ispec(lambda t, *_: (jnp.minimum(t + 2, nt - 1), 0, 0)),
                  ispec(lambda t, *_: (t, 0, 0)), ispec(lambda t, *_: (t, 0, 0)),
                  pl.BlockSpec(memory_space=pl.ANY),
                  wspec((d, f), 0), wspec((d, f), 0), wspec((f, d), 0),
                  wspec((d, f), 1), wspec((d, f), 1), wspec((f, d), 1)],
        out_specs=pl.BlockSpec(memory_space=pl.ANY),
        scratch_shapes=[pltpu.VMEM((GATHER_AHEAD + 1, tm, dx), F32), pltpu.VMEM((2, tm, d), F32),
                        pltpu.SemaphoreType.DMA((GATHER_AHEAD + 1,)), pltpu.SemaphoreType.DMA((2,))])
    return pl.pallas_call(
        _expert_kernel,
        grid_spec=gs,
        out_shape=jax.ShapeDtypeStruct((n + 2 * tm, d), F32),
        compiler_params=_cparams("arbitrary"),
        name="experts",
    )(elo, ehi, nrows, gidx, gidx, gidx, sprev, sidx, hx, w_gate, w_up, w_down, w_gate, w_up, w_down)


def _qkv_kernel(x_ref, y_ref, gate_ref, g_ref, sh_ref, sc_ref, w_ref, xo_ref, q_ref, k_ref, v_ref, *, qscale):
    x = x_ref[0] + gate_ref[0] * y_ref[...]
    xo_ref[0] = x
    h = _norm_mod(x, g_ref[...], sh_ref[0], sc_ref[0])
    r = _bdot(h, w_ref[...])
    d = x.shape[-1]
    npair = q_ref.shape[1]
    lanes = q_ref.shape[-1]
    for p in range(npair):
        q_ref[0, p] = (r[:, p * lanes:(p + 1) * lanes] * qscale).astype(BF16)
        k_ref[0, p] = r[:, d + p * lanes:d + (p + 1) * lanes].astype(BF16)
        v_ref[0, p] = r[:, 2 * d + p * lanes:2 * d + (p + 1) * lanes].astype(BF16)


def _flat_rows(row0, t, tm, d):
    return pl.BlockSpec((tm, d), lambda bi, i: (row0 // tm + bi * (t // tm) + i, 0))


def _qkv(x, y, row0, gate, g, shift, scale, w_qkv):
    b, t, d = x.shape
    tm = min(TOKEN_TILE, t)
    assert row0 % tm == 0
    npair = NA_HEADS // 2
    lanes = d // npair
    tok = pl.BlockSpec((1, tm, d), lambda bi, i: (bi, i, 0))
    row = pl.BlockSpec((1, 1, d), lambda bi, i: (bi, 0, 0))
    hp = pl.BlockSpec((1, npair, tm, lanes), lambda bi, i: (bi, 0, i, 0))
    hps = jax.ShapeDtypeStruct((b, npair, t, lanes), BF16)
    return pl.pallas_call(
        functools.partial(_qkv_kernel, qscale=float((d // NA_HEADS) ** -0.5)),
        grid=(b, t // tm),
        in_specs=[tok, _flat_rows(row0, t, tm, d), row, pl.BlockSpec((1, d), lambda bi, i: (0, 0)), row, row,
                  pl.BlockSpec(w_qkv.shape, lambda bi, i: (0, 0))],
        out_specs=[tok, hp, hp, hp],
        out_shape=[jax.ShapeDtypeStruct((b, t, d), F32), hps, hps, hps],
        compiler_params=_cparams("parallel", "parallel"),
        name="qkv",
    )(x, y, gate, g, shift, scale, w_qkv)


def _na_bias_tables(rpb):
    h = rpb.shape[0]
    col = jnp.arange(GRID_W)
    c_start = jnp.clip(col - NA_KC // 2, 0, GRID_W - NA_KC)
    col_ok = (col[None, :] >= c_start[:, None]) & (col[None, :] < c_start[:, None] + NA_KC)
    dc_idx = jnp.clip(col[None, :] - col[:, None] + NA_KC - 1, 0, 2 * NA_KC - 2)
    rc = jnp.where(col_ok[None, None], rpb[:, :, dc_idx].astype(F32), NEG_INF)
    dv = jnp.arange(NA_KR)[:, None]
    kr = jnp.arange(NA_KR)[None, :]
    t = rc[:, kr - dv + NA_KR - 1]
    t = t.reshape(h // 2, 2, NA_KR, NA_KR, GRID_W, GRID_W)
    return jnp.transpose(t, (0, 2, 1, 4, 3, 5)).reshape(h // 2, NA_KR, 2 * GRID_W, NA_KR * GRID_W)


def _attn_kernel(q_ref, kbuf, vbuf, kx_ref, vx_ref, bias_ref, o_ref,
                 q2_ref, sc_ref, pl_ref, pc_ref, li_ref, *, rows):
    i = pl.program_id(2)
    blk = q_ref.shape[2]
    kbuf, vbuf = kbuf.at[0, 0], vbuf.at[0, 0]
    lanes = q_ref.shape[-1]
    w2 = 2 * GRID_W
    lane = lax.broadcasted_iota(jnp.int32, (GRID_W, lanes), 1)
    first = lane < lanes // 2
    nt_dims = (((1,), (1,)), ((), ()))
    for rho in range(NA_KR):
        q = q_ref[0, 0, rho * GRID_W:(rho + 1) * GRID_W, :]
        q2_ref[rho * w2:rho * w2 + GRID_W, :] = jnp.where(first, q, jnp.zeros_like(q))
        q2_ref[rho * w2 + GRID_W:(rho + 1) * w2, :] = jnp.where(first, jnp.zeros_like(q), q)
    sc_ref[...] = lax.dot_general(q2_ref[...], kx_ref[0, 0], nt_dims, preferred_element_type=F32)

    def window(rho):
        r = NA_KR * i + rho
        rs = jnp.clip(r - NA_KR // 2, 0, rows - NA_KR)
        return pl.multiple_of(rs * GRID_W, GRID_W), r - rs

    for rho in range(NA_KR):
        wstart, dvar = window(rho)
        sl = slice(rho * w2, (rho + 1) * w2)
        s_loc = lax.dot_general(q2_ref[sl, :], kbuf[pl.ds(wstart, blk), :], nt_dims,
                                preferred_element_type=F32) + bias_ref[0, dvar]
        s_ctx = sc_ref[sl, :]
        m = jnp.maximum(jnp.max(s_loc, axis=-1, keepdims=True), jnp.max(s_ctx, axis=-1, keepdims=True))
        p_loc = jnp.exp(s_loc - m)
        p_ctx = jnp.exp(s_ctx - m)
        den = jnp.sum(p_loc, axis=-1, keepdims=True) + jnp.sum(p_ctx, axis=-1, keepdims=True)
        pl_ref[rho] = p_loc.astype(BF16)
        pc_ref[sl, :] = p_ctx.astype(BF16)
        li_ref[sl, :] = jnp.broadcast_to(1.0 / den, (w2, lanes))

    o_ctx = jnp.dot(pc_ref[...], vx_ref[0, 0], preferred_element_type=F32)
    for rho in range(NA_KR):
        wstart, _ = window(rho)
        sl = slice(rho * w2, (rho + 1) * w2)
        o = jnp.dot(pl_ref[rho], vbuf[pl.ds(wstart, blk), :], preferred_element_type=F32)
        o = (o + o_ctx[sl, :]) * li_ref[sl, :]
        o_ref[0, rho * GRID_W:(rho + 1) * GRID_W, :] = jnp.where(first, o[:GRID_W], o[GRID_W:]).astype(o_ref.dtype)


def _attention(q, k, v, kx, vx, bias):
    b, npair, t, lanes = q.shape
    c = kx.shape[2]
    blk = NA_KR * GRID_W
    nb = t // blk
    rows = t // GRID_W
    whole = pl.BlockSpec((1, 1, t, lanes), lambda p, bi, i: (bi, p, 0, 0))
    cx = pl.BlockSpec((1, 1, c, lanes), lambda p, bi, i: (bi, p, 0, 0))
    return pl.pallas_call(
        functools.partial(_attn_kernel, rows=rows),
        grid=(npair, b, nb),
        in_specs=[pl.BlockSpec((1, 1, blk, lanes), lambda p, bi, i: (bi, p, i, 0)), whole, whole, cx, cx,
                  pl.BlockSpec((1,) + bias.shape[1:], lambda p, bi, i: (p, 0, 0, 0))],
        out_specs=pl.BlockSpec((1, blk, lanes), lambda p, bi, i: (bi, i, p)),
        out_shape=jax.ShapeDtypeStruct((b, t, npair * lanes), BF16),
        scratch_shapes=[pltpu.VMEM((2 * blk, lanes), BF16), pltpu.VMEM((2 * blk, c), F32),
                        pltpu.VMEM((NA_KR, 2 * GRID_W, blk), BF16), pltpu.VMEM((2 * blk, c), BF16),
                        pltpu.VMEM((2 * blk, lanes), F32)],
        compiler_params=_cparams("parallel", "parallel", "parallel"),
        name="na_attention",
    )(q, k, v, kx, vx, bias)


def _oproj_kernel(x_ref, o_ref, wo_ref, gate_ref, g2_ref, sh_ref, sc_ref, wr_ref, br_ref, xo_ref, h_ref, rec_ref):
    subs = _sub_rows(x_ref.shape[1])
    y = [jnp.dot(o_ref[0, r, :], wo_ref[...], preferred_element_type=F32) for r in subs]
    h = []
    for r, v in zip(subs, y):
        x = x_ref[0, r, :] + gate_ref[0] * v
        xo_ref[0, r, :] = x
        h.append(_norm_mod(x, g2_ref[...], sh_ref[0], sc_ref[0]))
    logits = [_route_logits(v, wr_ref, br_ref) for v in h]
    for r, v, lg in zip(subs, h, logits):
        _route_select(v, lg, h_ref, rec_ref, r)


def _oproj(x, o, w_out, gate, g2, shift, scale, wr, br):
    b, t, d = x.shape
    tm = min(TOKEN_TILE, t)
    tok = lambda n: pl.BlockSpec((1, tm, n), lambda bi, i: (bi, i, 0))
    row = pl.BlockSpec((1, 1, d), lambda bi, i: (bi, 0, 0))
    full = lambda shape: pl.BlockSpec(shape, lambda bi, i: (0,) * len(shape))
    return pl.pallas_call(
        _oproj_kernel,
        grid=(b, t // tm),
        in_specs=[tok(d), tok(d), full((d, d)), row, full((1, d)), row, row, full(wr.shape), full(br.shape)],
        out_specs=[tok(d), tok(d + ROUTE_LANES), pl.BlockSpec((1, REC_FIELDS, tm), lambda bi, i: (bi, 0, i))],
        out_shape=[jax.ShapeDtypeStruct((b, t, d), F32), jax.ShapeDtypeStruct((b, t, d + ROUTE_LANES), F32),
                   jax.ShapeDtypeStruct((b, REC_FIELDS, t), F32)],
        compiler_params=_cparams("parallel", "parallel"),
        name="oproj_route",
    )(x, o, w_out, gate, g2, shift, scale, wr, br)


def _final_kernel(x_ref, y_ref, gate_ref, g_ref, o_ref):
    x = x_ref[0] + gate_ref[0] * y_ref[...]
    ms = jnp.mean(x * x, axis=-1, keepdims=True)
    o_ref[0] = (x * lax.rsqrt(ms + RMS_EPS)) * g_ref[...]


def _final(x, y, gate, g):
    b, t, d = x.shape
    tm = min(WIDE_TILE, t)
    tok = pl.BlockSpec((1, tm, d), lambda bi, i: (bi, i, 0))
    return pl.pallas_call(
        _final_kernel,
        grid=(b, t // tm),
        in_specs=[tok, _flat_rows(0, t, tm, d), pl.BlockSpec((1, 1, d), lambda bi, i: (bi, 0, 0)),
                  pl.BlockSpec((1, d), lambda bi, i: (0, 0))],
        out_specs=tok,
        out_shape=jax.ShapeDtypeStruct((b, t, d), F32),
        compiler_params=_cparams("parallel", "parallel"),
        name="final_norm",
    )(x, y, gate, g)


def kernel(x, c, ctx, c_ctx, ada_w, ada_b, norm1_g, norm2_g, rec_w_in, rec_conv_w, rec_conv_b, lru_wa, lru_ba, lru_wx, lru_bx, lru_lambda, s5_a_re, s5_a_im, s5_log_dt, s5_b_re, s5_b_im, s5_c_re, s5_c_im, s5_d, s5_glu_w, s5_glu_b, rec_w_out, na_w_qkv, na_w_out, na_rpb, moe_r1_w, moe_r1_b, moe_r2_w, moe_r2_b, moe_w_gate, moe_w_up, moe_w_down, final_norm_g):
    b, t, d = x.shape
    tc = ctx.shape[1]
    assert ada_w.shape[0] == 2, "layer 0 recurrent mixer, layer 1 neighbourhood attention"
    w = rec_w_in.shape[-1] // 3

    rpad = -(b + 1) % 8
    cc = jnp.concatenate([c, c_ctx[None], jnp.zeros((rpad, d), F32)], axis=0)
    mod = _ada_mod(cc, ada_w, ada_b)

    def mods(layer, ctx_rows):
        rows = jnp.broadcast_to(mod[layer, b:b + 1], (b, 6 * d)) if ctx_rows else mod[layer, :b]
        return [rows[:, j * d:(j + 1) * d].reshape(b, 1, d) for j in range(6)]

    row = lambda v: v.reshape(1, -1)

    w_in = rec_w_in[0].astype(BF16)
    lru = [(_block_diag(lru_wa[0, dr]).astype(BF16), row(lru_ba[0, dr]), _block_diag(lru_wx[0, dr]).astype(BF16),
            row(lru_bx[0, dr]), row(lru_lambda[0, dr])) for dr in (0, 1)]
    s5t = _s5_tables(s5_a_re[0], s5_a_im[0], s5_log_dt[0], s5_b_re[0], s5_b_im[0], s5_c_re[0], s5_c_im[0])
    glu_w = s5_glu_w[0].astype(BF16)
    w_out0 = rec_w_out[0].astype(BF16)
    wr0, br0 = _router_tables(moe_r1_w[0], moe_r1_b[0], moe_r2_w[0], moe_r2_b[0])

    def mixer0(xs, m, h0_lru, h0_s5):
        xa, ga, ub = _inproj(xs, row(norm1_g[0]), m[0], m[1], w_in)
        hf = _lru_dir(xa, rec_conv_w[0], row(rec_conv_b[0]), *lru[0], h0_lru[0], False)
        hb = _lru_dir(xa, rec_conv_w[0], row(rec_conv_b[0]), *lru[1], h0_lru[1], True)
        ys, s5_fin = _s5(ub, s5t, h0_s5)
        x_mid, h2, rec = _merge(xs, ga, hf, hb, ub, ys, row(s5_d[0]), glu_w, row(s5_glu_b[0]), w_out0, m[2],
                                row(norm2_g[0]), m[3], m[4], wr0, br0)
        return x_mid, h2, rec, (hf[:, -1:], hb[:, :1]), s5_fin

    zl = jnp.zeros((b, 1, w), F32)
    zs = jnp.zeros((s5t[0].shape[0], b, s5t[0].shape[-1]), F32)
    mc0, ml0 = mods(0, True), mods(0, False)
    xc_mid, hc2, recc, lru_fin, s5_fin = mixer0(ctx, mc0, (zl, zl), zs)
    xl_mid, hl2, recl, _, _ = mixer0(x, ml0, lru_fin, s5_fin)

    h_all = jnp.concatenate([hl2.reshape(b * t, -1), hc2.reshape(b * tc, -1)], axis=0)
    field = lambda k: jnp.concatenate([recl[:, k].reshape(b * t), recc[:, k].reshape(b * tc)])
    y_all = _experts(h_all, field(2), field(3), moe_w_gate[0].astype(BF16), moe_w_up[0].astype(BF16),
                     moe_w_down[0].astype(BF16))

    w_qkv = na_w_qkv[0].astype(BF16)
    mc1, ml1 = mods(1, True), mods(1, False)
    _, _, kx, vx = _qkv(xc_mid, y_all, b * t, mc0[5], row(norm1_g[1]), mc1[0], mc1[1], w_qkv)
    x1, q, k, v = _qkv(xl_mid, y_all, 0, ml0[5], row(norm1_g[1]), ml1[0], ml1[1], w_qkv)
    o = _attention(q, k, v, kx, vx, _na_bias_tables(na_rpb[0]))
    wr1, br1 = _router_tables(moe_r1_w[1], moe_r1_b[1], moe_r2_w[1], moe_r2_b[1])
    x1_mid, h2, rec = _oproj(x1, o, na_w_out[0].astype(BF16), ml1[2], row(norm2_g[1]), ml1[3], ml1[4], wr1, br1)
    y1 = _experts(h2.reshape(b * t, -1), rec[:, 2].reshape(b * t), rec[:, 3].reshape(b * t), moe_w_gate[1].astype(BF16),
                  moe_w_up[1].astype(BF16), moe_w_down[1].astype(BF16))
    return _final(x1_mid, y1, ml1[5], row(final_norm_g))
```
